```python
import jax, jax.numpy as jnp
from jax import lax
import numpy as np

D_MODEL = 1024
BATCH = 4
SEQ = 8192
DEPTH = 1

HEAD_DIM = 64
N_SB_HEADS = 8
N_SW_HEADS = 8
N_SW_KV_HEADS = 2
D_SB = N_SB_HEADS * HEAD_DIM
D_SW = N_SW_HEADS * HEAD_DIM
D_SW_KV = N_SW_KV_HEADS * HEAD_DIM
D_MIX = D_SB + D_SW
D_IN = 3 * D_SB + D_SW + 2 * D_SW_KV
IN_SPLITS = (D_SB, 2 * D_SB, 3 * D_SB, 3 * D_SB + D_SW, 3 * D_SB + D_SW + D_SW_KV)
Q_BLOCK = 128
WINDOW = 128
ROPE_THETA = 10000.0
N_GROUPS = 4
EXPERTS_PER_GROUP = 4
N_EXPERTS = N_GROUPS * EXPERTS_PER_GROUP
TOP_K = 2
D_EXPERT = 512
EXPERT_BLOCK = 128
EPS = 1e-6

kernel_name = "hymba_sb_swa_hmoe_adaln_block"


def rmsnorm(x, g):
    xf = x.astype(jnp.float32)
    y = xf * lax.rsqrt(jnp.mean(xf * xf, axis=-1, keepdims=True) + EPS)
    return (y * g.astype(jnp.float32)).astype(x.dtype)


def rope(x, positions):
    d = x.shape[-1]
    inv_freq = ROPE_THETA ** (-jnp.arange(0, d, 2, dtype=jnp.float32) / d)
    ang = positions.astype(jnp.float32)[..., None] * inv_freq
    cos = jnp.cos(ang)[:, :, None, :]
    sin = jnp.sin(ang)[:, :, None, :]
    xf = x.astype(jnp.float32)
    x1, x2 = xf[..., : d // 2], xf[..., d // 2:]
    return jnp.concatenate([x1 * cos - x2 * sin, x2 * cos + x1 * sin], axis=-1).astype(x.dtype)


def stick_breaking_attention(q, k, v):
    B, S, H, d = q.shape
    nblk = S // Q_BLOCK
    scale = d ** -0.5
    qb = q.reshape(B, nblk, Q_BLOCK, H, d).transpose(1, 0, 2, 3, 4)
    kf = k.astype(jnp.float32)
    vf = v.astype(jnp.float32)
    key_pos = jnp.arange(S)

    def one_block(args):
        i, qi = args
        z = jnp.einsum('bqhd,bkhd->bhqk', qi.astype(jnp.float32), kf) * scale
        q_pos = i * Q_BLOCK + jnp.arange(Q_BLOCK)
        causal = key_pos[None, :] < q_pos[:, None]
        log_beta = jax.nn.log_sigmoid(z)
        log_one_minus = jnp.where(causal, jax.nn.log_sigmoid(-z), 0.0)
        suffix = lax.cumsum(log_one_minus, axis=3, reverse=True) - log_one_minus
        weights = jnp.where(causal, jnp.exp(log_beta + suffix), 0.0)
        return jnp.einsum('bhqk,bkhd->bqhd', weights, vf)

    out = lax.map(one_block, (jnp.arange(nblk), qb))
    return out.transpose(1, 0, 2, 3, 4).reshape(B, S, H, d).astype(q.dtype)


def sliding_window_attention(q, k, v, sinks):
    B, S, Hq, d = q.shape
    Hkv = k.shape[2]
    rep = Hq // Hkv
    nblk = S // WINDOW
    scale = d ** -0.5
    qb = q.reshape(B, nblk, WINDOW, Hkv, rep, d)

    def with_prev(t):
        tb = t.reshape(B, nblk, WINDOW, Hkv, d)
        prev = jnp.pad(tb, ((0, 0), (1, 0), (0, 0), (0, 0), (0, 0)))[:, :-1]
        return jnp.concatenate([prev, tb], axis=2)

    kb = with_prev(k)
    vb = with_prev(v)
    s = jnp.einsum('bnqgrd,bnkgd->bngrqk', qb, kb,
                   preferred_element_type=jnp.float32) * scale
    qi = jnp.arange(WINDOW)[:, None]
    kk = jnp.arange(2 * WINDOW)[None, :]
    diff = WINDOW + qi - kk
    in_window = (diff >= 0) & (diff < WINDOW)
    blk = jnp.arange(nblk)[:, None, None]
    key_exists = blk * WINDOW + kk[None] - WINDOW >= 0
    mask = (in_window[None] & key_exists)[None, :, None, None]
    s = jnp.where(mask, s, -1e30)
    sink = sinks.astype(jnp.float32).reshape(1, 1, Hkv, rep, 1, 1)
    m = jnp.maximum(jnp.max(s, axis=-1, keepdims=True), sink)
    p = jnp.exp(s - m)
    denom = jnp.sum(p, axis=-1, keepdims=True) + jnp.exp(sink - m)
    p = (p / denom).astype(v.dtype)
    out = jnp.einsum('bngrqk,bnkgd->bnqgrd', p, vb)
    return out.reshape(B, S, Hq, d)


def hierarchical_moe(h, w_router_group, w_router_expert, w_gate, w_up, w_down):
    B, S, D = h.shape
    T = B * S
    xt = h.reshape(T, D)
    group_logits = jnp.einsum('td,dg->tg', xt, w_router_group, preferred_element_type=jnp.float32)
    group_probs = jax.nn.softmax(group_logits, axis=-1)
    g_sel = jnp.argmax(group_logits, axis=-1)
    g_prob = jnp.take_along_axis(group_probs, g_sel[:, None], axis=-1)
    expert_logits = jnp.einsum('td,de->te', xt, w_router_expert,
                               preferred_element_type=jnp.float32).reshape(T, N_GROUPS, EXPERTS_PER_GROUP)
    within = jnp.take_along_axis(expert_logits, g_sel[:, None, None], axis=1)[:, 0]
    top_logits, top_local = lax.top_k(within, TOP_K)
    top_w = jax.nn.softmax(top_logits, axis=-1) * g_prob
    expert_id = g_sel[:, None] * EXPERTS_PER_GROUP + top_local

    A = T * TOP_K
    flat_e = expert_id.reshape(A).astype(jnp.int32)
    flat_w = top_w.reshape(A)
    flat_tok = jnp.arange(A, dtype=jnp.int32) // TOP_K
    order = jnp.argsort(flat_e)
    sorted_e = flat_e[order]
    counts = jax.ops.segment_sum(jnp.ones_like(flat_e), flat_e, num_segments=N_EXPERTS)
    padded = (counts + EXPERT_BLOCK - 1) // EXPERT_BLOCK * EXPERT_BLOCK
    start = jnp.cumsum(counts) - counts
    pstart = jnp.cumsum(padded) - padded
    dest = pstart[sorted_e] + (jnp.arange(A, dtype=jnp.int32) - start[sorted_e])
    P = A + N_EXPERTS * EXPERT_BLOCK
    nb = P // EXPERT_BLOCK
    buf_tok = jnp.zeros((P,), jnp.int32).at[dest].set(flat_tok[order])
    buf_w = jnp.zeros((P,), jnp.float32).at[dest].set(flat_w[order])
    block_start = jnp.arange(nb, dtype=jnp.int32) * EXPERT_BLOCK
    block_e = jnp.minimum(jnp.searchsorted(pstart + padded, block_start, side='right'),
                          N_EXPERTS - 1).astype(jnp.int32)

    def expert_block(args):
        tok, e = args
        xb = xt[tok]
        hidden = jax.nn.silu(xb @ w_gate[e]) * (xb @ w_up[e])
        return hidden @ w_down[e]

    yb = lax.map(expert_block, (buf_tok.reshape(nb, EXPERT_BLOCK), block_e))
    y = jax.ops.segment_sum(yb.reshape(P, D) * buf_w[:, None].astype(yb.dtype), buf_tok,
                            num_segments=T)
    return y.reshape(B, S, D).astype(h.dtype)


def setup_inputs(seed: int = 0) -> dict:
    key = jax.random.key(seed)
    ks = jax.random.split(key, 20)

    def nrm(k, shape, s):
        return jax.random.normal(k, shape, jnp.float32) * s

    x = nrm(ks[0], (BATCH, SEQ, D_MODEL), 1.0)
    c = nrm(ks[1], (BATCH, D_MODEL), 1.0)
    offsets = jax.random.randint(ks[2], (BATCH, 1), 0, 4096, dtype=jnp.int32)
    positions = offsets + jnp.arange(SEQ, dtype=jnp.int32)[None, :]
    return {
        "x": x,
        "c": c,
        "positions": positions,
        "w_ada": nrm(ks[3], (DEPTH, D_MODEL, 6 * D_MODEL), D_MODEL ** -0.5),
        "b_ada": nrm(ks[4], (DEPTH, 6 * D_MODEL), 0.01),
        "norm_mix_g": 1.0 + nrm(ks[5], (DEPTH, D_MODEL), 0.02),
        "w_in": nrm(ks[6], (DEPTH, D_MODEL, D_IN), D_MODEL ** -0.5),
        "sinks": nrm(ks[7], (DEPTH, N_SW_HEADS), 1.0),
        "out_norm_sb_g": 1.0 + nrm(ks[8], (DEPTH, D_SB), 0.02),
        "out_norm_sw_g": 1.0 + nrm(ks[9], (DEPTH, D_SW), 0.02),
        "w_out": nrm(ks[10], (DEPTH, D_MIX, D_MODEL), D_MIX ** -0.5),
        "norm_ffn_g": 1.0 + nrm(ks[11], (DEPTH, D_MODEL), 0.02),
        "w_router_group": nrm(ks[12], (DEPTH, D_MODEL, N_GROUPS), D_MODEL ** -0.5),
        "w_router_expert": nrm(ks[13], (DEPTH, D_MODEL, N_EXPERTS), D_MODEL ** -0.5),
        "w_gate": nrm(ks[14], (DEPTH, N_EXPERTS, D_MODEL, D_EXPERT), D_MODEL ** -0.5),
        "w_up": nrm(ks[15], (DEPTH, N_EXPERTS, D_MODEL, D_EXPERT), D_MODEL ** -0.5),
        "w_down": nrm(ks[16], (DEPTH, N_EXPERTS, D_EXPERT, D_MODEL), D_EXPERT ** -0.5),
        "norm_final_g": 1.0 + nrm(ks[17], (D_MODEL,), 0.02),
    }


def reference(x, c, positions, w_ada, b_ada, norm_mix_g, w_in, sinks, out_norm_sb_g, out_norm_sw_g,
              w_out, norm_ffn_g, w_router_group, w_router_expert, w_gate, w_up, w_down, norm_final_g):
    B, S, D = x.shape
    for l in range(DEPTH):
        mod = jax.nn.silu(c) @ w_ada[l] + b_ada[l]
        shift1, scale1, gate1, shift2, scale2, gate2 = jnp.split(mod[:, None, :], 6, axis=-1)

        h = rmsnorm(x, norm_mix_g[l]) * (1.0 + scale1) + shift1
        proj = jnp.einsum('bsd,de->bse', h, w_in[l])
        q_sb, k_sb, v_sb, q_sw, k_sw, v_sw = jnp.split(proj, IN_SPLITS, axis=-1)
        o_sb = stick_breaking_attention(q_sb.reshape(B, S, N_SB_HEADS, HEAD_DIM),
                                        k_sb.reshape(B, S, N_SB_HEADS, HEAD_DIM),
                                        v_sb.reshape(B, S, N_SB_HEADS, HEAD_DIM))
        q_sw = rope(q_sw.reshape(B, S, N_SW_HEADS, HEAD_DIM), positions)
        k_sw = rope(k_sw.reshape(B, S, N_SW_KV_HEADS, HEAD_DIM), positions)
        o_sw = sliding_window_attention(q_sw, k_sw, v_sw.reshape(B, S, N_SW_KV_HEADS, HEAD_DIM), sinks[l])
        mixed = jnp.concatenate([rmsnorm(o_sb.reshape(B, S, D_SB), out_norm_sb_g[l]),
                                 rmsnorm(o_sw.reshape(B, S, D_SW), out_norm_sw_g[l])], axis=-1)
        x = x + gate1 * jnp.einsum('bse,ed->bsd', mixed, w_out[l])

        h2 = rmsnorm(x, norm_ffn_g[l]) * (1.0 + scale2) + shift2
        x = x + gate2 * hierarchical_moe(h2, w_router_group[l], w_router_expert[l],
                                         w_gate[l], w_up[l], w_down[l])
    return rmsnorm(x, norm_final_g)
```

```python
import functools

import jax
import jax.numpy as jnp
from jax import lax
from jax.experimental import pallas as pl
from jax.experimental.pallas import tpu as pltpu

F32 = jnp.float32
BF16 = jnp.bfloat16

HEAD_DIM = 64
N_SB_HEADS = 8
N_SW_HEADS = 8
N_SW_KV_HEADS = 2
WINDOW = 128
ROPE_THETA = 10000.0
N_GROUPS = 4
EXPERTS_PER_GROUP = 4
N_PAIRS = 6
N_BUCKETS = N_GROUPS * N_PAIRS
EPS = 1e-6

LANES = 128
VMEM_LIMIT = 56 * 1024 * 1024
SB_SKIP_LOG = -104.0

TM_PROJ = 512
SB_BLOCK = 256
ROW_BLOCK = 256
TM_ROUTE = 512
TM_COMB = 256
META_ROWS = 32


def _cparams(sem, vmem=VMEM_LIMIT):
    return pltpu.CompilerParams(dimension_semantics=sem, vmem_limit_bytes=vmem)


def _adaln_kernel(c_ref, w_ref, b_ref, o_ref):
    c = c_ref[...]
    s = c * jax.nn.sigmoid(c)
    o_ref[...] = jnp.dot(s, w_ref[...], precision=lax.Precision.HIGHEST,
                         preferred_element_type=F32) + b_ref[...]


def _adaln(c_pad, w, b):
    rows, d = c_pad.shape
    n = w.shape[1]
    tn = 1024
    return pl.pallas_call(
        _adaln_kernel,
        grid=(n // tn,),
        in_specs=[pl.BlockSpec((rows, d), lambda j: (0, 0)),
                  pl.BlockSpec((d, tn), lambda j: (0, j)),
                  pl.BlockSpec((1, tn), lambda j: (0, j))],
        out_specs=pl.BlockSpec((rows, tn), lambda j: (0, j)),
        out_shape=jax.ShapeDtypeStruct((rows, n), F32),
        compiler_params=_cparams(("parallel",)),
        name="adaln",
    )(c_pad, w, b)


def _rope_kernel(pos_ref, freq_ref, sign_ref, cos_ref, sin_ref):
    ang = pos_ref[...].astype(F32) * freq_ref[...]
    cos_ref[...] = jnp.cos(ang)
    sin_ref[...] = jnp.sin(ang) * sign_ref[...]


def _rope_tables(pos_b, freq, sign):
    t = pos_b.shape[0]
    tm = 1024
    row = pl.BlockSpec((tm, LANES), lambda i: (i, 0))
    vec = pl.BlockSpec((1, LANES), lambda i: (0, 0))
    return pl.pallas_call(
        _rope_kernel,
        grid=(t // tm,),
        in_specs=[row, vec, vec],
        out_specs=[row, row],
        out_shape=[jax.ShapeDtypeStruct((t, LANES), F32)] * 2,
        compiler_params=_cparams(("parallel",)),
        name="rope",
    )(pos_b, freq, sign)


def _rms(x):
    return x * lax.rsqrt(jnp.mean(x * x, axis=-1, keepdims=True) + EPS)


def _rotate_half_pairs(x):
    lane = lax.broadcasted_iota(jnp.int32, x.shape, 1)
    first_half = (lane % HEAD_DIM) < (HEAD_DIM // 2)
    return jnp.where(first_half, pltpu.roll(x, LANES - HEAD_DIM // 2, 1), pltpu.roll(x, HEAD_DIM // 2, 1))


def _inproj_kernel(d_sb3, d_swq, x_ref, mod_ref, g_ref, w_ref, cos_ref, sin_ref, sb_ref, q_ref, kv_ref):
    x = x_ref[...]
    shift = mod_ref[0, 0:1, :]
    scale = mod_ref[0, 1:2, :]
    h = (_rms(x) * g_ref[...]) * (1.0 + scale) + shift
    hb = h.astype(BF16)
    chunk = 512
    for c0 in range(0, d_sb3, chunk):
        sb_ref[:, c0:c0 + chunk] = jnp.dot(hb, w_ref[:, c0:c0 + chunk],
                                           preferred_element_type=F32).astype(BF16)
    cos = cos_ref[...]
    sin = sin_ref[...]
    q = jnp.dot(hb, w_ref[:, d_sb3:d_sb3 + d_swq], preferred_element_type=F32)
    for c0 in range(0, d_swq, LANES):
        qt = q[:, c0:c0 + LANES]
        q_ref[:, c0:c0 + LANES] = (qt * cos + _rotate_half_pairs(qt) * sin).astype(BF16)
    kv = jnp.dot(hb, w_ref[:, d_sb3 + d_swq:], preferred_element_type=F32)
    k = kv[:, :LANES]
    kv_ref[:, :LANES] = (k * cos + _rotate_half_pairs(k) * sin).astype(BF16)
    kv_ref[:, LANES:] = kv[:, LANES:].astype(BF16)


def _inproj(x2, mod3, g, w_bf, cos_t, sin_t, tiles_per_batch, d_sb3, d_swq):
    t, d = x2.shape
    d_in = w_bf.shape[1]
    d_kv = d_in - d_sb3 - d_swq
    tm = TM_PROJ
    return pl.pallas_call(
        functools.partial(_inproj_kernel, d_sb3, d_swq),
        grid=(t // tm,),
        in_specs=[pl.BlockSpec((tm, d), lambda i: (i, 0)),
                  pl.BlockSpec((1, 6, d), lambda i: (i // tiles_per_batch, 0, 0)),
                  pl.BlockSpec((1, d), lambda i: (0, 0)),
                  pl.BlockSpec((d, d_in), lambda i: (0, 0)),
                  pl.BlockSpec((tm, LANES), lambda i: (i, 0)),
                  pl.BlockSpec((tm, LANES), lambda i: (i, 0))],
        out_specs=[pl.BlockSpec((tm, d_sb3), lambda i: (i, 0)),
                   pl.BlockSpec((tm, d_swq), lambda i: (i, 0)),
                   pl.BlockSpec((tm, d_kv), lambda i: (i, 0))],
        out_shape=[jax.ShapeDtypeStruct((t, d_sb3), BF16),
                   jax.ShapeDtypeStruct((t, d_swq), BF16),
                   jax.ShapeDtypeStruct((t, d_kv), BF16)],
        compiler_params=_cparams(("parallel",)),
        name="inproj",
    )(x2, mod3, g, w_bf, cos_t, sin_t)


def _dot_nt(a, b):
    return lax.dot_general(a, b, (((1,), (1,)), ((), ())), preferred_element_type=F32)


def _sb_block(qm, k, v, tri, carry, causal):
    z = _dot_nt(qm, k)
    log_beta = jnp.minimum(z, 0.0) - jnp.log(1.0 + jnp.exp(-jnp.abs(z)))
    log_rest = log_beta - z
    if causal is not None:
        log_rest = jnp.where(causal, log_rest, 0.0)
    hi = log_rest.astype(BF16)
    lo = (log_rest - hi.astype(F32)).astype(BF16)
    later = jnp.dot(hi, tri, preferred_element_type=F32) + jnp.dot(lo, tri, preferred_element_type=F32)
    w = jnp.exp(log_beta + later + carry)
    if causal is not None:
        w = jnp.where(causal, w, 0.0)
    out = jnp.dot(w.astype(BF16), v, preferred_element_type=F32)
    return out, carry + jnp.sum(log_rest, axis=1, keepdims=True)


def _sb_kernel(blk, q_ref, k_ref, v_ref, tri_ref, o_ref):
    i = pl.program_id(2)
    q = q_ref[0]
    tri = tri_ref[...]
    lane = lax.broadcasted_iota(jnp.int32, q.shape, 1)
    row = lax.broadcasted_iota(jnp.int32, (blk, blk), 0)
    col = lax.broadcasted_iota(jnp.int32, (blk, blk), 1)
    causal = col < row
    accs = []
    for head in range(2):
        in_head = (lane >= head * HEAD_DIM) & (lane < (head + 1) * HEAD_DIM)
        qm = jnp.where(in_head, q, jnp.zeros_like(q))
        start = pl.multiple_of(i * blk, blk)
        acc, carry = _sb_block(qm, k_ref[0, pl.ds(start, blk), :], v_ref[0, pl.ds(start, blk), :], tri,
                               jnp.zeros((blk, 1), F32), causal)

        def cond(state):
            kb, carry, _ = state
            return jnp.logical_and(kb >= 0, jnp.max(carry) >= SB_SKIP_LOG)

        def body(state):
            kb, carry, acc = state
            s = pl.multiple_of(kb * blk, blk)
            out, carry = _sb_block(qm, k_ref[0, pl.ds(s, blk), :], v_ref[0, pl.ds(s, blk), :], tri, carry, None)
            return kb - 1, carry, acc + out

        _, _, acc = lax.while_loop(cond, body, (i - 1, carry, acc))
        accs.append(acc)
    o_ref[0] = jnp.where(lane < HEAD_DIM, accs[0], accs[1]).astype(o_ref.dtype)


def _sb_attention(qkv3, tri):
    b, s, d3 = qkv3.shape
    d_sb = d3 // 3
    n_pairs = d_sb // LANES
    blk = SB_BLOCK
    return pl.pallas_call(
        functools.partial(_sb_kernel, blk),
        grid=(b, n_pairs, s // blk),
        in_specs=[pl.BlockSpec((1, blk, LANES), lambda bi, p, i: (bi, i, p)),
                  pl.BlockSpec((1, s, LANES), lambda bi, p, i: (bi, 0, n_pairs + p)),
                  pl.BlockSpec((1, s, LANES), lambda bi, p, i: (bi, 0, 2 * n_pairs + p)),
                  pl.BlockSpec((blk, blk), lambda bi, p, i: (0, 0))],
        out_specs=pl.BlockSpec((1, blk, LANES), lambda bi, p, i: (bi, i, p)),
        out_shape=jax.ShapeDtypeStruct((b, s, d_sb), BF16),
        compiler_params=_cparams(("parallel", "parallel", "arbitrary")),
        name="sb_attn",
    )(qkv3, qkv3, qkv3, tri)


def _sw_kernel(n_tiles, sinks_ref, q_ref, kvp_ref, kvc_ref, o_ref):
    i = pl.program_id(1)
    w = WINDOW
    kk = jnp.concatenate([kvp_ref[0, :, :LANES], kvc_ref[0, :, :LANES]], axis=0)
    vv = jnp.concatenate([kvp_ref[0, :, LANES:], kvc_ref[0, :, LANES:]], axis=0)
    qi = lax.broadcasted_iota(jnp.int32, (w, 2 * w), 0)
    kj = lax.broadcasted_iota(jnp.int32, (w, 2 * w), 1)
    diff = w + qi - kj
    mask = (diff >= 0) & (diff < w) & (i * w + kj - w >= 0)
    lane = lax.broadcasted_iota(jnp.int32, (w, LANES), 1)
    for t in range(n_tiles):
        qt = q_ref[0, :, t * LANES:(t + 1) * LANES]
        outs = []
        for half in range(2):
            head = t + half * n_tiles
            in_head = (lane >= half * HEAD_DIM) & (lane < (half + 1) * HEAD_DIM)
            qm = jnp.where(in_head, qt, jnp.zeros_like(qt))
            s = jnp.where(mask, _dot_nt(qm, kk), -1e30)
            sink = sinks_ref[head]
            m = jnp.maximum(jnp.max(s, axis=-1, keepdims=True), sink)
            p = jnp.exp(s - m)
            denom = jnp.sum(p, axis=-1, keepdims=True) + jnp.exp(sink - m)
            outs.append(jnp.dot(p.astype(BF16), vv, preferred_element_type=F32) * (1.0 / denom))
        o_ref[0, :, t * LANES:(t + 1) * LANES] = jnp.where(lane < HEAD_DIM, outs[0], outs[1]).astype(o_ref.dtype)


def _sw_attention(sinks, q3, kv3):
    b, s, dq = q3.shape
    dkv = kv3.shape[2]
    w = WINDOW
    n_tiles = dq // LANES
    return pl.pallas_call(
        functools.partial(_sw_kernel, n_tiles),
        grid=(b, s // w),
        in_specs=[pl.BlockSpec(memory_space=pltpu.SMEM),
                  pl.BlockSpec((1, w, dq), lambda bi, i: (bi, i, 0)),
                  pl.BlockSpec((1, w, dkv), lambda bi, i: (bi, jnp.maximum(i - 1, 0), 0)),
                  pl.BlockSpec((1, w, dkv), lambda bi, i: (bi, i, 0))],
        out_specs=pl.BlockSpec((1, w, dq), lambda bi, i: (bi, i, 0)),
        out_shape=jax.ShapeDtypeStruct((b, s, dq), BF16),
        compiler_params=_cparams(("parallel", "parallel")),
        name="sw_attn",
    )(sinks, q3, kv3, kv3)


def _first_max(vals):
    m = vals[0]
    for v in vals[1:]:
        m = jnp.maximum(m, v)
    idx = jnp.full(m.shape, len(vals) - 1, jnp.int32)
    for k in range(len(vals) - 2, -1, -1):
        idx = jnp.where(vals[k] == m, k, idx)
    return m, idx


def _route(lg):
    g = [lg[k:k + 1, :] for k in range(N_GROUPS)]
    gmax, gsel = _first_max(g)
    gsum = g[0] * 0.0
    for k in range(N_GROUPS):
        gsum = gsum + jnp.exp(g[k] - gmax)
    gprob = 1.0 / gsum
    within = []
    for k in range(EXPERTS_PER_GROUP):
        v = lg[N_GROUPS + (N_GROUPS - 1) * EXPERTS_PER_GROUP + k:N_GROUPS + (N_GROUPS - 1) * EXPERTS_PER_GROUP + k + 1, :]
        for grp in range(N_GROUPS - 2, -1, -1):
            r = N_GROUPS + grp * EXPERTS_PER_GROUP + k
            v = jnp.where(gsel == grp, lg[r:r + 1, :], v)
        within.append(v)
    m1, i1 = _first_max(within)
    rest = [jnp.where(i1 == k, -jnp.inf, within[k]) for k in range(EXPERTS_PER_GROUP)]
    m2, i2 = _first_max(rest)
    e2 = jnp.exp(m2 - m1)
    p1 = gprob / (1.0 + e2)
    p2 = gprob * e2 / (1.0 + e2)
    lo = jnp.minimum(i1, i2)
    hi = jnp.maximum(i1, i2)
    w_lo = jnp.where(i1 < i2, p1, p2)
    w_hi = jnp.where(i1 < i2, p2, p1)
    pair = jnp.where(lo == 0, hi - 1, jnp.where(lo == 1, hi + 1, 5))
    return gsel * N_PAIRS + pair, w_lo, w_hi


def _outproj_kernel(d_sb, osb_ref, osw_ref, x_ref, mod_ref, gsb_ref, gsw_ref, wo_ref, gffn_ref, wr_ref, tri_ref,
                    x1_ref, h2_ref, meta_ref, cnt_ref, run_ref):
    i = pl.program_id(0)
    tm = x_ref.shape[0]
    d = x_ref.shape[1]

    @pl.when(i == 0)
    def _():
        run_ref[...] = jnp.zeros_like(run_ref)

    gate1 = mod_ref[0, 2:3, :]
    shift2 = mod_ref[0, 3:4, :]
    scale2 = mod_ref[0, 4:5, :]
    a = (_rms(osb_ref[...].astype(F32)) * gsb_ref[...]).astype(BF16)
    bsw = (_rms(osw_ref[...].astype(F32)) * gsw_ref[...]).astype(BF16)
    mix = jnp.dot(a, wo_ref[:d_sb, :], preferred_element_type=F32) + jnp.dot(bsw, wo_ref[d_sb:, :],
                                                                                preferred_element_type=F32)
    x1 = x_ref[...] + gate1 * mix
    x1_ref[...] = x1
    h2 = (_rms(x1) * gffn_ref[...]) * (1.0 + scale2) + shift2
    h2_ref[:, :d] = h2
    lg = lax.dot_general(wr_ref[...], h2, (((1,), (1,)), ((), ())), precision=lax.Precision.HIGHEST,
                         preferred_element_type=F32)
    bucket, w_lo, w_hi = _route(lg)
    rows = lax.broadcasted_iota(jnp.int32, (META_ROWS, tm), 0)
    onehot = rows == bucket
    prefix = jnp.dot(jnp.where(onehot, 1.0, 0.0).astype(BF16), tri_ref[...], preferred_element_type=F32)
    run = run_ref[...]
    rank = jnp.sum(jnp.where(onehot, prefix - 1.0 + run, 0.0), axis=0, keepdims=True)
    run = run + jnp.sum(jnp.where(onehot, 1.0, 0.0), axis=1, keepdims=True)
    run_ref[...] = run
    cnt_ref[...] = jnp.broadcast_to(run, cnt_ref.shape)
    meta_ref[...] = jnp.concatenate([bucket, rank.astype(jnp.int32), jnp.zeros((6, tm), jnp.int32)], axis=0)
    wrows = jnp.concatenate([w_lo, w_hi, jnp.zeros((LANES - 2, tm), F32)], axis=0)
    h2_ref[:, d:] = wrows.T


def _outproj(o_sb, o_sw, x2, mod3, g_sb, g_sw, wo_bf, g_ffn, wr, tri, tiles_per_batch):
    t, d = x2.shape
    d_sb = o_sb.shape[1]
    d_sw = o_sw.shape[1]
    tm = TM_ROUTE
    const = lambda shape: pl.BlockSpec(shape, lambda i: tuple(0 for _ in shape))
    return pl.pallas_call(
        functools.partial(_outproj_kernel, d_sb),
        grid=(t // tm,),
        in_specs=[pl.BlockSpec((tm, d_sb), lambda i: (i, 0)),
                  pl.BlockSpec((tm, d_sw), lambda i: (i, 0)),
                  pl.BlockSpec((tm, d), lambda i: (i, 0)),
                  pl.BlockSpec((1, 6, d), lambda i: (i // tiles_per_batch, 0, 0)),
                  const((1, d_sb)), const((1, d_sw)), const((d_sb + d_sw, d)), const((1, d)),
                  const((META_ROWS, d)), const((tm, tm))],
        out_specs=[pl.BlockSpec((tm, d), lambda i: (i, 0)),
                   pl.BlockSpec((tm, d + LANES), lambda i: (i, 0)),
                   pl.BlockSpec((8, tm), lambda i: (0, i)),
                   const((META_ROWS, LANES))],
        out_shape=[jax.ShapeDtypeStruct((t, d), F32),
                   jax.ShapeDtypeStruct((t, d + LANES), F32),
                   jax.ShapeDtypeStruct((8, t), jnp.int32),
                   jax.ShapeDtypeStruct((META_ROWS, LANES), F32)],
        scratch_shapes=[pltpu.VMEM((META_ROWS, 1), F32)],
        compiler_params=_cparams(("arbitrary",)),
        name="outproj",
    )(o_sb, o_sw, x2, mod3, g_sb, g_sw, wo_bf, g_ffn, wr, tri)


def _row_copy(src_ref, src_row, dst_ref, dst_row, sem):
    return pltpu.make_async_copy(src_ref.at[pl.ds(src_row, 1), :], dst_ref.at[pl.ds(dst_row, 1), :], sem)


def _dispatch_kernel(dest_ref, h_ref, zeros_ref, out_ref, sem):
    del zeros_ref
    tm = h_ref.shape[0]
    base = pl.program_id(0) * tm

    def issue(r, c):
        _row_copy(h_ref, r, out_ref, dest_ref[base + r], sem).start()
        return c

    lax.fori_loop(0, tm, issue, 0)

    def drain(r, c):
        _row_copy(h_ref, 0, out_ref, 0, sem).wait()
        return c

    lax.fori_loop(0, tm, drain, 0)


def _dispatch(dest, h2e, zeros_sorted):
    t, de = h2e.shape
    tm = TM_ROUTE
    return pl.pallas_call(
        _dispatch_kernel,
        grid_spec=pltpu.PrefetchScalarGridSpec(
            num_scalar_prefetch=1,
            grid=(t // tm,),
            in_specs=[pl.BlockSpec((tm, de), lambda i, dest: (i, 0)),
                      pl.BlockSpec(memory_space=pl.ANY)],
            out_specs=pl.BlockSpec(memory_space=pl.ANY),
            scratch_shapes=[pltpu.SemaphoreType.DMA(())]),
        out_shape=jax.ShapeDtypeStruct(zeros_sorted.shape, zeros_sorted.dtype),
        input_output_aliases={2: 0},
        compiler_params=_cparams(("arbitrary",)),
        name="dispatch",
    )(dest, h2e, zeros_sorted)


def _experts_kernel(d, ea_ref, eb_ref, nvalid_ref, x_ref, wga_ref, wua_ref, wda_ref, wgb_ref, wub_ref, wdb_ref, y_ref):
    j = pl.program_id(0)

    @pl.when(j < nvalid_ref[0])
    def _():
        xe = x_ref[...]
        xb = xe[:, :d].astype(BF16)

        def hidden(wg_ref, wu_ref, wrow):
            gt = jnp.dot(xb, wg_ref[0], preferred_element_type=F32)
            up = jnp.dot(xb, wu_ref[0], preferred_element_type=F32)
            return ((gt * jax.nn.sigmoid(gt)) * up * wrow).astype(BF16)

        ha = hidden(wga_ref, wua_ref, xe[:, d:d + 1])
        hb = hidden(wgb_ref, wub_ref, xe[:, d + 1:d + 2])
        y_ref[...] = (jnp.dot(ha, wda_ref[0], preferred_element_type=F32)
                      + jnp.dot(hb, wdb_ref[0], preferred_element_type=F32))

    @pl.when(j >= nvalid_ref[0])
    def _():
        y_ref[...] = jnp.zeros_like(y_ref)


def _experts(blk_ea, blk_eb, nvalid, sorted_rows, wg_bf, wu_bf, wd_bf):
    p, de = sorted_rows.shape
    n_e, d, d_e = wg_bf.shape
    rb = ROW_BLOCK
    nb = p // rb
    up_a = pl.BlockSpec((1, d, d_e), lambda j, ea, eb, nv: (ea[j], 0, 0))
    up_b = pl.BlockSpec((1, d, d_e), lambda j, ea, eb, nv: (eb[j], 0, 0))
    dn_a = pl.BlockSpec((1, d_e, d), lambda j, ea, eb, nv: (ea[j], 0, 0))
    dn_b = pl.BlockSpec((1, d_e, d), lambda j, ea, eb, nv: (eb[j], 0, 0))
    return pl.pallas_call(
        functools.partial(_experts_kernel, d),
        grid_spec=pltpu.PrefetchScalarGridSpec(
            num_scalar_prefetch=3,
            grid=(nb,),
            in_specs=[pl.BlockSpec((rb, de), lambda j, ea, eb, nv: (j, 0)),
                      up_a, up_a, dn_a, up_b, up_b, dn_b],
            out_specs=pl.BlockSpec((rb, d), lambda j, ea, eb, nv: (j, 0))),
        out_shape=jax.ShapeDtypeStruct((p, d), F32),
        compiler_params=_cparams(("arbitrary",)),
        name="experts",
    )(blk_ea, blk_eb, nvalid, sorted_rows, wg_bf, wu_bf, wd_bf, wg_bf, wu_bf, wd_bf)


def _combine_kernel(dest_ref, x1_ref, mod_ref, g_ref, y_hbm, o_ref, ybuf, sems):
    i = pl.program_id(0)
    n = pl.num_programs(0)
    tm = x1_ref.shape[0]
    slot = i % 2

    def issue(tile, slot_):
        def one(r, c):
            _row_copy(y_hbm, dest_ref[tile * tm + r], ybuf.at[slot_], r, sems.at[slot_]).start()
            return c
        lax.fori_loop(0, tm, one, 0)

    @pl.when(i == 0)
    def _():
        issue(0, 0)

    @pl.when(i + 1 < n)
    def _():
        issue(i + 1, 1 - slot)

    def drain(r, c):
        _row_copy(y_hbm, 0, ybuf.at[slot], 0, sems.at[slot]).wait()
        return c

    lax.fori_loop(0, tm, drain, 0)
    gate2 = mod_ref[0, 5:6, :]
    o_ref[...] = _rms(x1_ref[...] + gate2 * ybuf[slot]) * g_ref[...]


def _combine(dest, x1, mod3, g_final, y_sorted, tiles_per_batch):
    t, d = x1.shape
    tm = TM_COMB
    return pl.pallas_call(
        _combine_kernel,
        grid_spec=pltpu.PrefetchScalarGridSpec(
            num_scalar_prefetch=1,
            grid=(t // tm,),
            in_specs=[pl.BlockSpec((tm, d), lambda i, dest: (i, 0)),
                      pl.BlockSpec((1, 6, d), lambda i, dest: (i // tiles_per_batch, 0, 0)),
                      pl.BlockSpec((1, d), lambda i, dest: (0, 0)),
                      pl.BlockSpec(memory_space=pl.ANY)],
            out_specs=pl.BlockSpec((tm, d), lambda i, dest: (i, 0)),
            scratch_shapes=[pltpu.VMEM((2, tm, d), F32), pltpu.SemaphoreType.DMA((2,))]),
        out_shape=jax.ShapeDtypeStruct((t, d), F32),
        compiler_params=_cparams(("arbitrary",)),
        name="combine",
    )(dest, x1, mod3, g_final, y_sorted)


def _pair_table():
    lo, hi = [], []
    for g in range(N_GROUPS):
        for a in range(EXPERTS_PER_GROUP):
            for b in range(a + 1, EXPERTS_PER_GROUP):
                lo.append(g * EXPERTS_PER_GROUP + a)
                hi.append(g * EXPERTS_PER_GROUP + b)
    return jnp.array(lo, jnp.int32), jnp.array(hi, jnp.int32)


def _layer(x, mod3, positions, norm_mix_g, w_in, sinks, out_norm_sb_g, out_norm_sw_g, w_out, norm_ffn_g,
           w_router_group, w_router_expert, w_gate, w_up, w_down, norm_final_g):
    b, s, d = x.shape
    t = b * s
    d_sb = N_SB_HEADS * HEAD_DIM
    d_sw = N_SW_HEADS * HEAD_DIM
    d_kv = N_SW_KV_HEADS * HEAD_DIM
    scale = HEAD_DIM ** -0.5
    x2 = x.reshape(t, d)

    n_tiles = d_sw // LANES
    sw_heads = jnp.array([h for j in range(n_tiles) for h in (j, j + n_tiles)], jnp.int32)
    sw_cols = (sw_heads[:, None] * HEAD_DIM + jnp.arange(HEAD_DIM, dtype=jnp.int32)[None, :]).reshape(-1)
    q_sw_cols = 3 * d_sb + sw_cols
    w_bf = jnp.concatenate([w_in[:, :d_sb] * scale, w_in[:, d_sb:3 * d_sb],
                            w_in[:, q_sw_cols] * scale, w_in[:, 3 * d_sb + d_sw:]], axis=1).astype(BF16)

    inv_freq = ROPE_THETA ** (-jnp.arange(0, HEAD_DIM, 2, dtype=F32) / HEAD_DIM)
    freq = jnp.tile(inv_freq, LANES // (HEAD_DIM // 2)).reshape(1, LANES)
    sign = jnp.tile(jnp.concatenate([-jnp.ones(HEAD_DIM // 2, F32), jnp.ones(HEAD_DIM // 2, F32)]),
                    LANES // HEAD_DIM).reshape(1, LANES)
    pos_b = jnp.broadcast_to(positions.reshape(t, 1), (t, LANES))
    cos_t, sin_t = _rope_tables(pos_b, freq, sign)

    qkv_sb, q_sw, kv_sw = _inproj(x2, mod3, norm_mix_g.reshape(1, d), w_bf, cos_t, sin_t, s // TM_PROJ,
                                  3 * d_sb, d_sw)

    tri_sb = (jnp.arange(SB_BLOCK)[:, None] > jnp.arange(SB_BLOCK)[None, :]).astype(BF16)
    o_sb = _sb_attention(qkv_sb.reshape(b, s, 3 * d_sb), tri_sb).reshape(t, d_sb)
    o_sw = _sw_attention(sinks, q_sw.reshape(b, s, d_sw), kv_sw.reshape(b, s, 2 * d_kv)).reshape(t, d_sw)

    wo_bf = jnp.concatenate([w_out[:d_sb], w_out[d_sb + sw_cols]], axis=0).astype(BF16)
    wr = jnp.concatenate([w_router_group.T, w_router_expert.T,
                          jnp.zeros((META_ROWS - N_GROUPS - N_GROUPS * EXPERTS_PER_GROUP, d), F32)], axis=0)
    tri_rt = (jnp.arange(TM_ROUTE)[:, None] <= jnp.arange(TM_ROUTE)[None, :]).astype(BF16)
    x1, h2e, meta, cnt = _outproj(o_sb, o_sw, x2, mod3, out_norm_sb_g.reshape(1, d_sb),
                                  out_norm_sw_g[sw_cols].reshape(1, d_sw), wo_bf, norm_ffn_g.reshape(1, d),
                                  wr, tri_rt, s // TM_ROUTE)

    counts = cnt[:N_BUCKETS, 0].astype(jnp.int32)
    padded = (counts + ROW_BLOCK - 1) // ROW_BLOCK * ROW_BLOCK
    pend = jnp.cumsum(padded)
    pstart = pend - padded
    dest = pstart[meta[0]] + meta[1]
    p_rows = t + N_BUCKETS * ROW_BLOCK
    nb = p_rows // ROW_BLOCK
    blk_bucket = jnp.minimum(jnp.searchsorted(pend, jnp.arange(nb, dtype=jnp.int32) * ROW_BLOCK, side='right'),
                             N_BUCKETS - 1).astype(jnp.int32)
    e_lo, e_hi = _pair_table()
    nvalid = (pend[-1:] // ROW_BLOCK).astype(jnp.int32)

    sorted_rows = _dispatch(dest, h2e, jnp.zeros((p_rows, d + LANES), F32))
    y_sorted = _experts(e_lo[blk_bucket], e_hi[blk_bucket], nvalid, sorted_rows,
                        w_gate.astype(BF16), w_up.astype(BF16), w_down.astype(BF16))
    return _combine(dest, x1, mod3, norm_final_g.reshape(1, d), y_sorted, s // TM_COMB).reshape(b, s, d)


def kernel(x, c, positions, w_ada, b_ada, norm_mix_g, w_in, sinks, out_norm_sb_g, out_norm_sw_g, w_out, norm_ffn_g,
           w_router_group, w_router_expert, w_gate, w_up, w_down, norm_final_g):
    b, s, d = x.shape
    depth = w_ada.shape[0]
    assert depth == 1, "the final norm is fused into the layer's last stage"
    c_pad = jnp.pad(c, ((0, 8 - b), (0, 0)))
    mod = _adaln(c_pad, w_ada[0], b_ada[0].reshape(1, -1))
    mod3 = mod[:b].reshape(b, 6, d)
    return _layer(x, mod3, positions, norm_mix_g[0], w_in[0], sinks[0], out_norm_sb_g[0], out_norm_sw_g[0],
                  w_out[0], norm_ffn_g[0], w_router_group[0], w_router_expert[0], w_gate[0], w_up[0], w_down[0],
                  norm_final_g)
```

```python
import functools

import jax
import jax.numpy as jnp
from jax import lax
from jax.experimental import pallas as pl
from jax.experimental.pallas import tpu as pltpu

F32 = jnp.float32
BF16 = jnp.bfloat16

HEAD_DIM = 64
N_SB_HEADS = 8
N_SW_HEADS = 8
N_SW_KV_HEADS = 2
WINDOW = 128
ROPE_THETA = 10000.0
N_GROUPS = 4
EXPERTS_PER_GROUP = 4
N_PAIRS = 6
N_BUCKETS = N_GROUPS * N_PAIRS
EPS = 1e-6

LANES = 128
VMEM_LIMIT = 56 * 1024 * 1024
LOG2_E = 1.4426950408889634
SB_SKIP_LOG2 = -104.0 * LOG2_E

TM_PROJ = 512
SB_BLOCK = 256
ROW_BLOCK = 256
TM_ROUTE = 512
TM_COMB = 256
META_ROWS = 32


def _cparams(sem, vmem=VMEM_LIMIT):
    return pltpu.CompilerParams(dimension_semantics=sem, vmem_limit_bytes=vmem)


def _adaln_kernel(c_ref, w_ref, b_ref, o_ref):
    c = c_ref[...]
    s = c * jax.nn.sigmoid(c)
    o_ref[...] = jnp.dot(s, w_ref[...], precision=lax.Precision.HIGHEST,
                         preferred_element_type=F32) + b_ref[...]


def _adaln(c_pad, w, b):
    rows, d = c_pad.shape
    n = w.shape[1]
    tn = 1024
    return pl.pallas_call(
        _adaln_kernel,
        grid=(n // tn,),
        in_specs=[pl.BlockSpec((rows, d), lambda j: (0, 0)),
                  pl.BlockSpec((d, tn), lambda j: (0, j)),
                  pl.BlockSpec((1, tn), lambda j: (0, j))],
        out_specs=pl.BlockSpec((rows, tn), lambda j: (0, j)),
        out_shape=jax.ShapeDtypeStruct((rows, n), F32),
        compiler_params=_cparams(("parallel",)),
        name="adaln",
    )(c_pad, w, b)


def _rope_kernel(pos_ref, freq_ref, sign_ref, cos_ref, sin_ref):
    ang = pos_ref[...].astype(F32) * freq_ref[...]
    cos_ref[...] = jnp.cos(ang)
    sin_ref[...] = jnp.sin(ang) * sign_ref[...]


def _rope_tables(pos_b, freq, sign):
    t = pos_b.shape[0]
    tm = 1024
    row = pl.BlockSpec((tm, LANES), lambda i: (i, 0))
    vec = pl.BlockSpec((1, LANES), lambda i: (0, 0))
    return pl.pallas_call(
        _rope_kernel,
        grid=(t // tm,),
        in_specs=[row, vec, vec],
        out_specs=[row, row],
        out_shape=[jax.ShapeDtypeStruct((t, LANES), F32)] * 2,
        compiler_params=_cparams(("parallel",)),
        name="rope",
    )(pos_b, freq, sign)


def _rms(x):
    return x * lax.rsqrt(jnp.mean(x * x, axis=-1, keepdims=True) + EPS)


def _rotate_half_pairs(x):
    lane = lax.broadcasted_iota(jnp.int32, x.shape, 1)
    first_half = (lane % HEAD_DIM) < (HEAD_DIM // 2)
    return jnp.where(first_half, pltpu.roll(x, LANES - HEAD_DIM // 2, 1), pltpu.roll(x, HEAD_DIM // 2, 1))


def _inproj_kernel(d_sb3, d_swq, x_ref, mod_ref, g_ref, w_ref, cos_ref, sin_ref, sb_ref, q_ref, kv_ref):
    x = x_ref[...]
    shift = mod_ref[0, 0:1, :]
    scale = mod_ref[0, 1:2, :]
    h = (_rms(x) * g_ref[...]) * (1.0 + scale) + shift
    hb = h.astype(BF16)
    chunk = 512
    for c0 in range(0, d_sb3, chunk):
        sb_ref[:, c0:c0 + chunk] = jnp.dot(hb, w_ref[:, c0:c0 + chunk],
                                           preferred_element_type=F32).astype(BF16)
    cos = cos_ref[...]
    sin = sin_ref[...]
    q = jnp.dot(hb, w_ref[:, d_sb3:d_sb3 + d_swq], preferred_element_type=F32)
    for c0 in range(0, d_swq, LANES):
        qt = q[:, c0:c0 + LANES]
        q_ref[:, c0:c0 + LANES] = (qt * cos + _rotate_half_pairs(qt) * sin).astype(BF16)
    kv = jnp.dot(hb, w_ref[:, d_sb3 + d_swq:], preferred_element_type=F32)
    k = kv[:, :LANES]
    kv_ref[:, :LANES] = (k * cos + _rotate_half_pairs(k) * sin).astype(BF16)
    kv_ref[:, LANES:] = kv[:, LANES:].astype(BF16)


def _inproj(x2, mod3, g, w_bf, cos_t, sin_t, tiles_per_batch, d_sb3, d_swq):
    t, d = x2.shape
    d_in = w_bf.shape[1]
    d_kv = d_in - d_sb3 - d_swq
    tm = TM_PROJ
    return pl.pallas_call(
        functools.partial(_inproj_kernel, d_sb3, d_swq),
        grid=(t // tm,),
        in_specs=[pl.BlockSpec((tm, d), lambda i: (i, 0)),
                  pl.BlockSpec((1, 6, d), lambda i: (i // tiles_per_batch, 0, 0)),
                  pl.BlockSpec((1, d), lambda i: (0, 0)),
                  pl.BlockSpec((d, d_in), lambda i: (0, 0)),
                  pl.BlockSpec((tm, LANES), lambda i: (i, 0)),
                  pl.BlockSpec((tm, LANES), lambda i: (i, 0))],
        out_specs=[pl.BlockSpec((tm, d_sb3), lambda i: (i, 0)),
                   pl.BlockSpec((tm, d_swq), lambda i: (i, 0)),
                   pl.BlockSpec((tm, d_kv), lambda i: (i, 0))],
        out_shape=[jax.ShapeDtypeStruct((t, d_sb3), BF16),
                   jax.ShapeDtypeStruct((t, d_swq), BF16),
                   jax.ShapeDtypeStruct((t, d_kv), BF16)],
        compiler_params=_cparams(("parallel",)),
        name="inproj",
    )(x2, mod3, g, w_bf, cos_t, sin_t)


def _dot_nt(a, b):
    return lax.dot_general(a, b, (((1,), (1,)), ((), ())), preferred_element_type=F32)


def _sb_block(qm, k, v, tri2, carry, causal):
    z = _dot_nt(qm, k)
    log_beta = jnp.minimum(z, 0.0) - jnp.log2(1.0 + jnp.exp2(-jnp.abs(z)))
    log_rest = log_beta - z
    if causal is not None:
        log_rest = jnp.where(causal, log_rest, 0.0)
    hi = log_rest.astype(BF16)
    lo = (log_rest - hi.astype(F32)).astype(BF16)
    later = jnp.dot(jnp.concatenate([hi, lo], axis=1), tri2, preferred_element_type=F32)
    w = jnp.exp2(log_beta + later + carry)
    if causal is not None:
        w = jnp.where(causal, w, 0.0)
    out = jnp.dot(w.astype(BF16), v, preferred_element_type=F32)
    return out, carry + jnp.sum(log_rest, axis=1, keepdims=True)


def _sb_kernel(blk, q_ref, k_ref, v_ref, tri_ref, o_ref):
    i = pl.program_id(2)
    q = q_ref[0]
    tri2 = tri_ref[...]
    lane = lax.broadcasted_iota(jnp.int32, q.shape, 1)
    row = lax.broadcasted_iota(jnp.int32, (blk, blk), 0)
    col = lax.broadcasted_iota(jnp.int32, (blk, blk), 1)
    causal = col < row
    qms = [jnp.where((lane >= h * HEAD_DIM) & (lane < (h + 1) * HEAD_DIM), q, jnp.zeros_like(q)) for h in range(2)]

    def sweep(kb, carries, causal_mask):
        s = pl.multiple_of(kb * blk, blk)
        k = k_ref[0, pl.ds(s, blk), :]
        v = v_ref[0, pl.ds(s, blk), :]
        return [_sb_block(qms[h], k, v, tri2, carries[h], causal_mask) for h in range(2)]

    diag = sweep(i, [jnp.zeros((blk, 1), F32)] * 2, causal)
    left = sweep(jnp.maximum(i - 1, 0), [jnp.where(i > 0, diag[h][1], -1e30) for h in range(2)], None)

    def cond(state):
        kb, c0, c1, _, _ = state
        return jnp.logical_and(kb >= 0, jnp.max(jnp.maximum(c0, c1)) >= SB_SKIP_LOG2)

    def body(state):
        kb, c0, c1, a0, a1 = state
        (o0, c0), (o1, c1) = sweep(kb, [c0, c1], None)
        return kb - 1, c0, c1, a0 + o0, a1 + o1

    _, _, _, a0, a1 = lax.while_loop(
        cond, body, (i - 2, left[0][1], left[1][1], diag[0][0] + left[0][0], diag[1][0] + left[1][0]))
    o_ref[0] = jnp.where(lane < HEAD_DIM, a0, a1).astype(o_ref.dtype)


def _sb_attention(qkv3, tri):
    b, s, d3 = qkv3.shape
    d_sb = d3 // 3
    n_pairs = d_sb // LANES
    blk = SB_BLOCK
    return pl.pallas_call(
        functools.partial(_sb_kernel, blk),
        grid=(b, n_pairs, s // blk),
        in_specs=[pl.BlockSpec((1, blk, LANES), lambda bi, p, i: (bi, i, p)),
                  pl.BlockSpec((1, s, LANES), lambda bi, p, i: (bi, 0, n_pairs + p)),
                  pl.BlockSpec((1, s, LANES), lambda bi, p, i: (bi, 0, 2 * n_pairs + p)),
                  pl.BlockSpec((2 * blk, blk), lambda bi, p, i: (0, 0))],
        out_specs=pl.BlockSpec((1, blk, LANES), lambda bi, p, i: (bi, i, p)),
        out_shape=jax.ShapeDtypeStruct((b, s, d_sb), BF16),
        compiler_params=_cparams(("parallel", "parallel", "arbitrary")),
        name="sb_attn",
    )(qkv3, qkv3, qkv3, tri)


def _sw_kernel(n_tiles, sinks_ref, q_ref, kvp_ref, kvc_ref, o_ref):
    i = pl.program_id(1)
    w = WINDOW
    kk = jnp.concatenate([kvp_ref[0, :, :LANES], kvc_ref[0, :, :LANES]], axis=0)
    vv = jnp.concatenate([kvp_ref[0, :, LANES:], kvc_ref[0, :, LANES:]], axis=0)
    qi = lax.broadcasted_iota(jnp.int32, (w, 2 * w), 0)
    kj = lax.broadcasted_iota(jnp.int32, (w, 2 * w), 1)
    diff = w + qi - kj
    mask = (diff >= 0) & (diff < w) & (i * w + kj - w >= 0)
    lane = lax.broadcasted_iota(jnp.int32, (w, LANES), 1)
    for t in range(n_tiles):
        qt = q_ref[0, :, t * LANES:(t + 1) * LANES]
        outs = []
        for half in range(2):
            head = t + half * n_tiles
            in_head = (lane >= half * HEAD_DIM) & (lane < (half + 1) * HEAD_DIM)
            qm = jnp.where(in_head, qt, jnp.zeros_like(qt))
            s = jnp.where(mask, _dot_nt(qm, kk), -1e30)
            sink = sinks_ref[head]
            m = jnp.maximum(jnp.max(s, axis=-1, keepdims=True), sink)
            p = jnp.exp(s - m)
            denom = jnp.sum(p, axis=-1, keepdims=True) + jnp.exp(sink - m)
            outs.append(jnp.dot(p.astype(BF16), vv, preferred_element_type=F32) * (1.0 / denom))
        o_ref[0, :, t * LANES:(t + 1) * LANES] = jnp.where(lane < HEAD_DIM, outs[0], outs[1]).astype(o_ref.dtype)


def _sw_attention(sinks, q3, kv3):
    b, s, dq = q3.shape
    dkv = kv3.shape[2]
    w = WINDOW
    n_tiles = dq // LANES
    return pl.pallas_call(
        functools.partial(_sw_kernel, n_tiles),
        grid=(b, s // w),
        in_specs=[pl.BlockSpec(memory_space=pltpu.SMEM),
                  pl.BlockSpec((1, w, dq), lambda bi, i: (bi, i, 0)),
                  pl.BlockSpec((1, w, dkv), lambda bi, i: (bi, jnp.maximum(i - 1, 0), 0)),
                  pl.BlockSpec((1, w, dkv), lambda bi, i: (bi, i, 0))],
        out_specs=pl.BlockSpec((1, w, dq), lambda bi, i: (bi, i, 0)),
        out_shape=jax.ShapeDtypeStruct((b, s, dq), BF16),
        compiler_params=_cparams(("parallel", "parallel")),
        name="sw_attn",
    )(sinks, q3, kv3, kv3)


def _first_max(vals):
    m = vals[0]
    for v in vals[1:]:
        m = jnp.maximum(m, v)
    idx = jnp.full(m.shape, len(vals) - 1, jnp.int32)
    for k in range(len(vals) - 2, -1, -1):
        idx = jnp.where(vals[k] == m, k, idx)
    return m, idx


def _route(lg):
    g = [lg[k:k + 1, :] for k in range(N_GROUPS)]
    gmax, gsel = _first_max(g)
    gsum = g[0] * 0.0
    for k in range(N_GROUPS):
        gsum = gsum + jnp.exp(g[k] - gmax)
    gprob = 1.0 / gsum
    within = []
    for k in range(EXPERTS_PER_GROUP):
        v = lg[N_GROUPS + (N_GROUPS - 1) * EXPERTS_PER_GROUP + k:N_GROUPS + (N_GROUPS - 1) * EXPERTS_PER_GROUP + k + 1, :]
        for grp in range(N_GROUPS - 2, -1, -1):
            r = N_GROUPS + grp * EXPERTS_PER_GROUP + k
            v = jnp.where(gsel == grp, lg[r:r + 1, :], v)
        within.append(v)
    m1, i1 = _first_max(within)
    rest = [jnp.where(i1 == k, -jnp.inf, within[k]) for k in range(EXPERTS_PER_GROUP)]
    m2, i2 = _first_max(rest)
    e2 = jnp.exp(m2 - m1)
    p1 = gprob / (1.0 + e2)
    p2 = gprob * e2 / (1.0 + e2)
    lo = jnp.minimum(i1, i2)
    hi = jnp.maximum(i1, i2)
    w_lo = jnp.where(i1 < i2, p1, p2)
    w_hi = jnp.where(i1 < i2, p2, p1)
    pair = jnp.where(lo == 0, hi - 1, jnp.where(lo == 1, hi + 1, 5))
    return gsel * N_PAIRS + pair, w_lo, w_hi


def _outproj_kernel(d_sb, osb_ref, osw_ref, x_ref, mod_ref, gsb_ref, gsw_ref, wo_ref, gffn_ref, wr_ref, tri_ref,
                    x1_ref, h2_ref, meta_ref, cnt_ref, run_ref):
    i = pl.program_id(0)
    tm = x_ref.shape[0]
    d = x_ref.shape[1]

    @pl.when(i == 0)
    def _():
        run_ref[...] = jnp.zeros_like(run_ref)

    gate1 = mod_ref[0, 2:3, :]
    shift2 = mod_ref[0, 3:4, :]
    scale2 = mod_ref[0, 4:5, :]
    a = (_rms(osb_ref[...].astype(F32)) * gsb_ref[...]).astype(BF16)
    bsw = (_rms(osw_ref[...].astype(F32)) * gsw_ref[...]).astype(BF16)
    mix = jnp.dot(a, wo_ref[:d_sb, :], preferred_element_type=F32) + jnp.dot(bsw, wo_ref[d_sb:, :],
                                                                                preferred_element_type=F32)
    x1 = x_ref[...] + gate1 * mix
    x1_ref[...] = x1
    h2 = (_rms(x1) * gffn_ref[...]) * (1.0 + scale2) + shift2
    h2_ref[:, :d] = h2
    lg = lax.dot_general(wr_ref[...], h2, (((1,), (1,)), ((), ())), precision=lax.Precision.HIGHEST,
                         preferred_element_type=F32)
    bucket, w_lo, w_hi = _route(lg)
    rows = lax.broadcasted_iota(jnp.int32, (META_ROWS, tm), 0)
    onehot = rows == bucket
    prefix = jnp.dot(jnp.where(onehot, 1.0, 0.0).astype(BF16), tri_ref[...], preferred_element_type=F32)
    run = run_ref[...]
    rank = jnp.sum(jnp.where(onehot, prefix - 1.0 + run, 0.0), axis=0, keepdims=True)
    run = run + jnp.sum(jnp.where(onehot, 1.0, 0.0), axis=1, keepdims=True)
    run_ref[...] = run
    cnt_ref[...] = jnp.broadcast_to(run, cnt_ref.shape)
    meta_ref[...] = jnp.concatenate([bucket, rank.astype(jnp.int32), jnp.zeros((6, tm), jnp.int32)], axis=0)
    wrows = jnp.concatenate([w_lo, w_hi, jnp.zeros((LANES - 2, tm), F32)], axis=0)
    h2_ref[:, d:] = wrows.T


def _outproj(o_sb, o_sw, x2, mod3, g_sb, g_sw, wo_bf, g_ffn, wr, tri, tiles_per_batch):
    t, d = x2.shape
    d_sb = o_sb.shape[1]
    d_sw = o_sw.shape[1]
    tm = TM_ROUTE
    const = lambda shape: pl.BlockSpec(shape, lambda i: tuple(0 for _ in shape))
    return pl.pallas_call(
        functools.partial(_outproj_kernel, d_sb),
        grid=(t // tm,),
        in_specs=[pl.BlockSpec((tm, d_sb), lambda i: (i, 0)),
                  pl.BlockSpec((tm, d_sw), lambda i: (i, 0)),
                  pl.BlockSpec((tm, d), lambda i: (i, 0)),
                  pl.BlockSpec((1, 6, d), lambda i: (i // tiles_per_batch, 0, 0)),
                  const((1, d_sb)), const((1, d_sw)), const((d_sb + d_sw, d)), const((1, d)),
                  const((META_ROWS, d)), const((tm, tm))],
        out_specs=[pl.BlockSpec((tm, d), lambda i: (i, 0)),
                   pl.BlockSpec((tm, d + LANES), lambda i: (i, 0)),
                   pl.BlockSpec((8, tm), lambda i: (0, i)),
                   const((META_ROWS, LANES))],
        out_shape=[jax.ShapeDtypeStruct((t, d), F32),
                   jax.ShapeDtypeStruct((t, d + LANES), F32),
                   jax.ShapeDtypeStruct((8, t), jnp.int32),
                   jax.ShapeDtypeStruct((META_ROWS, LANES), F32)],
        scratch_shapes=[pltpu.VMEM((META_ROWS, 1), F32)],
        compiler_params=_cparams(("arbitrary",)),
        name="outproj",
    )(o_sb, o_sw, x2, mod3, g_sb, g_sw, wo_bf, g_ffn, wr, tri)


def _row_copy(src_ref, src_row, dst_ref, dst_row, sem):
    return pltpu.make_async_copy(src_ref.at[pl.ds(src_row, 1), :], dst_ref.at[pl.ds(dst_row, 1), :], sem)


def _dispatch_kernel(dest_ref, h_ref, zeros_ref, out_ref, sem):
    del zeros_ref
    tm = h_ref.shape[0]
    base = pl.program_id(0) * tm

    def issue(r, c):
        _row_copy(h_ref, r, out_ref, dest_ref[base + r], sem).start()
        return c

    lax.fori_loop(0, tm, issue, 0)

    def drain(r, c):
        _row_copy(h_ref, 0, out_ref, 0, sem).wait()
        return c

    lax.fori_loop(0, tm, drain, 0)


def _dispatch(dest, h2e, zeros_sorted):
    t, de = h2e.shape
    tm = TM_ROUTE
    return pl.pallas_call(
        _dispatch_kernel,
        grid_spec=pltpu.PrefetchScalarGridSpec(
            num_scalar_prefetch=1,
            grid=(t // tm,),
            in_specs=[pl.BlockSpec((tm, de), lambda i, dest: (i, 0)),
                      pl.BlockSpec(memory_space=pl.ANY)],
            out_specs=pl.BlockSpec(memory_space=pl.ANY),
            scratch_shapes=[pltpu.SemaphoreType.DMA(())]),
        out_shape=jax.ShapeDtypeStruct(zeros_sorted.shape, zeros_sorted.dtype),
        input_output_aliases={2: 0},
        compiler_params=_cparams(("arbitrary",)),
        name="dispatch",
    )(dest, h2e, zeros_sorted)


def _experts_kernel(d, ea_ref, eb_ref, nvalid_ref, x_ref, wga_ref, wua_ref, wda_ref, wgb_ref, wub_ref, wdb_ref, y_ref):
    j = pl.program_id(0)

    @pl.when(j < nvalid_ref[0])
    def _():
        xe = x_ref[...]
        xb = xe[:, :d].astype(BF16)

        def hidden(wg_ref, wu_ref, wrow):
            gt = jnp.dot(xb, wg_ref[0], preferred_element_type=F32)
            up = jnp.dot(xb, wu_ref[0], preferred_element_type=F32)
            return ((gt * jax.nn.sigmoid(gt)) * up * wrow).astype(BF16)

        ha = hidden(wga_ref, wua_ref, xe[:, d:d + 1])
        hb = hidden(wgb_ref, wub_ref, xe[:, d + 1:d + 2])
        y_ref[...] = (jnp.dot(ha, wda_ref[0], preferred_element_type=F32)
                      + jnp.dot(hb, wdb_ref[0], preferred_element_type=F32))

    @pl.when(j >= nvalid_ref[0])
    def _():
        y_ref[...] = jnp.zeros_like(y_ref)


def _experts(blk_ea, blk_eb, nvalid, sorted_rows, wg_bf, wu_bf, wd_bf):
    p, de = sorted_rows.shape
    n_e, d, d_e = wg_bf.shape
    rb = ROW_BLOCK
    nb = p // rb
    up_a = pl.BlockSpec((1, d, d_e), lambda j, ea, eb, nv: (ea[j], 0, 0))
    up_b = pl.BlockSpec((1, d, d_e), lambda j, ea, eb, nv: (eb[j], 0, 0))
    dn_a = pl.BlockSpec((1, d_e, d), lambda j, ea, eb, nv: (ea[j], 0, 0))
    dn_b = pl.BlockSpec((1, d_e, d), lambda j, ea, eb, nv: (eb[j], 0, 0))
    return pl.pallas_call(
        functools.partial(_experts_kernel, d),
        grid_spec=pltpu.PrefetchScalarGridSpec(
            num_scalar_prefetch=3,
            grid=(nb,),
            in_specs=[pl.BlockSpec((rb, de), lambda j, ea, eb, nv: (j, 0)),
                      up_a, up_a, dn_a, up_b, up_b, dn_b],
            out_specs=pl.BlockSpec((rb, d), lambda j, ea, eb, nv: (j, 0))),
        out_shape=jax.ShapeDtypeStruct((p, d), F32),
        compiler_params=_cparams(("arbitrary",)),
        name="experts",
    )(blk_ea, blk_eb, nvalid, sorted_rows, wg_bf, wu_bf, wd_bf, wg_bf, wu_bf, wd_bf)


def _combine_kernel(dest_ref, x1_ref, mod_ref, g_ref, y_hbm, o_ref, ybuf, sems):
    i = pl.program_id(0)
    n = pl.num_programs(0)
    tm = x1_ref.shape[0]
    slot = i % 2

    def issue(tile, slot_):
        def one(r, c):
            _row_copy(y_hbm, dest_ref[tile * tm + r], ybuf.at[slot_], r, sems.at[slot_]).start()
            return c
        lax.fori_loop(0, tm, one, 0)

    @pl.when(i == 0)
    def _():
        issue(0, 0)

    @pl.when(i + 1 < n)
    def _():
        issue(i + 1, 1 - slot)

    def drain(r, c):
        _row_copy(y_hbm, 0, ybuf.at[slot], 0, sems.at[slot]).wait()
        return c

    lax.fori_loop(0, tm, drain, 0)
    gate2 = mod_ref[0, 5:6, :]
    o_ref[...] = _rms(x1_ref[...] + gate2 * ybuf[slot]) * g_ref[...]


def _combine(dest, x1, mod3, g_final, y_sorted, tiles_per_batch):
    t, d = x1.shape
    tm = TM_COMB
    return pl.pallas_call(
        _combine_kernel,
        grid_spec=pltpu.PrefetchScalarGridSpec(
            num_scalar_prefetch=1,
            grid=(t // tm,),
            in_specs=[pl.BlockSpec((tm, d), lambda i, dest: (i, 0)),
                      pl.BlockSpec((1, 6, d), lambda i, dest: (i // tiles_per_batch, 0, 0)),
                      pl.BlockSpec((1, d), lambda i, dest: (0, 0)),
                      pl.BlockSpec(memory_space=pl.ANY)],
            out_specs=pl.BlockSpec((tm, d), lambda i, dest: (i, 0)),
            scratch_shapes=[pltpu.VMEM((2, tm, d), F32), pltpu.SemaphoreType.DMA((2,))]),
        out_shape=jax.ShapeDtypeStruct((t, d), F32),
        compiler_params=_cparams(("arbitrary",)),
        name="combine",
    )(dest, x1, mod3, g_final, y_sorted)


def _pair_table():
    lo, hi = [], []
    for g in range(N_GROUPS):
        for a in range(EXPERTS_PER_GROUP):
            for b in range(a + 1, EXPERTS_PER_GROUP):
                lo.append(g * EXPERTS_PER_GROUP + a)
                hi.append(g * EXPERTS_PER_GROUP + b)
    return jnp.array(lo, jnp.int32), jnp.array(hi, jnp.int32)


def _layer(x, mod3, positions, norm_mix_g, w_in, sinks, out_norm_sb_g, out_norm_sw_g, w_out, norm_ffn_g,
           w_router_group, w_router_expert, w_gate, w_up, w_down, norm_final_g):
    b, s, d = x.shape
    t = b * s
    d_sb = N_SB_HEADS * HEAD_DIM
    d_sw = N_SW_HEADS * HEAD_DIM
    d_kv = N_SW_KV_HEADS * HEAD_DIM
    scale = HEAD_DIM ** -0.5
    x2 = x.reshape(t, d)

    n_tiles = d_sw // LANES
    sw_heads = jnp.array([h for j in range(n_tiles) for h in (j, j + n_tiles)], jnp.int32)
    sw_cols = (sw_heads[:, None] * HEAD_DIM + jnp.arange(HEAD_DIM, dtype=jnp.int32)[None, :]).reshape(-1)
    q_sw_cols = 3 * d_sb + sw_cols
    w_bf = jnp.concatenate([w_in[:, :d_sb] * (scale * LOG2_E), w_in[:, d_sb:3 * d_sb],
                            w_in[:, q_sw_cols] * scale, w_in[:, 3 * d_sb + d_sw:]], axis=1).astype(BF16)

    inv_freq = ROPE_THETA ** (-jnp.arange(0, HEAD_DIM, 2, dtype=F32) / HEAD_DIM)
    freq = jnp.tile(inv_freq, LANES // (HEAD_DIM // 2)).reshape(1, LANES)
    sign = jnp.tile(jnp.concatenate([-jnp.ones(HEAD_DIM // 2, F32), jnp.ones(HEAD_DIM // 2, F32)]),
                    LANES // HEAD_DIM).reshape(1, LANES)
    pos_b = jnp.broadcast_to(positions.reshape(t, 1), (t, LANES))
    cos_t, sin_t = _rope_tables(pos_b, freq, sign)

    qkv_sb, q_sw, kv_sw = _inproj(x2, mod3, norm_mix_g.reshape(1, d), w_bf, cos_t, sin_t, s // TM_PROJ,
                                  3 * d_sb, d_sw)

    tri_sb = (jnp.arange(SB_BLOCK)[:, None] > jnp.arange(SB_BLOCK)[None, :]).astype(BF16)
    tri_sb = jnp.concatenate([tri_sb, tri_sb], axis=0)
    o_sb = _sb_attention(qkv_sb.reshape(b, s, 3 * d_sb), tri_sb).reshape(t, d_sb)
    o_sw = _sw_attention(sinks, q_sw.reshape(b, s, d_sw), kv_sw.reshape(b, s, 2 * d_kv)).reshape(t, d_sw)

    wo_bf = jnp.concatenate([w_out[:d_sb], w_out[d_sb + sw_cols]], axis=0).astype(BF16)
    wr = jnp.concatenate([w_router_group.T, w_router_expert.T,
                          jnp.zeros((META_ROWS - N_GROUPS - N_GROUPS * EXPERTS_PER_GROUP, d), F32)], axis=0)
    tri_rt = (jnp.arange(TM_ROUTE)[:, None] <= jnp.arange(TM_ROUTE)[None, :]).astype(BF16)
    x1, h2e, meta, cnt = _outproj(o_sb, o_sw, x2, mod3, out_norm_sb_g.reshape(1, d_sb),
                                  out_norm_sw_g[sw_cols].reshape(1, d_sw), wo_bf, norm_ffn_g.reshape(1, d),
                                  wr, tri_rt, s // TM_ROUTE)

    counts = cnt[:N_BUCKETS, 0].astype(jnp.int32)
    padded = (counts + ROW_BLOCK - 1) // ROW_BLOCK * ROW_BLOCK
    pend = jnp.cumsum(padded)
    pstart = pend - padded
    dest = pstart[meta[0]] + meta[1]
    p_rows = t + N_BUCKETS * ROW_BLOCK
    nb = p_rows // ROW_BLOCK
    blk_start = jnp.arange(nb, dtype=jnp.int32) * ROW_BLOCK
    blk_bucket = jnp.minimum(jnp.sum((pend[None, :] <= blk_start[:, None]).astype(jnp.int32), axis=1), N_BUCKETS - 1)
    e_lo, e_hi = _pair_table()
    nvalid = (pend[-1:] // ROW_BLOCK).astype(jnp.int32)

    sorted_rows = _dispatch(dest, h2e, jnp.zeros((p_rows, d + LANES), F32))
    y_sorted = _experts(e_lo[blk_bucket], e_hi[blk_bucket], nvalid, sorted_rows,
                        w_gate.astype(BF16), w_up.astype(BF16), w_down.astype(BF16))
    return _combine(dest, x1, mod3, norm_final_g.reshape(1, d), y_sorted, s // TM_COMB).reshape(b, s, d)


def kernel(x, c, positions, w_ada, b_ada, norm_mix_g, w_in, sinks, out_norm_sb_g, out_norm_sw_g, w_out, norm_ffn_g,
           w_router_group, w_router_expert, w_gate, w_up, w_down, norm_final_g):
    b, s, d = x.shape
    depth = w_ada.shape[0]
    assert depth == 1, "the final norm is fused into the layer's last stage"
    c_pad = jnp.pad(c, ((0, 8 - b), (0, 0)))
    mod = _adaln(c_pad, w_ada[0], b_ada[0].reshape(1, -1))
    mod3 = mod[:b].reshape(b, 6, d)
    return _layer(x, mod3, positions, norm_mix_g[0], w_in[0], sinks[0], out_norm_sb_g[0], out_norm_sw_g[0],
                  w_out[0], norm_ffn_g[0], w_router_group[0], w_router_expert[0], w_gate[0], w_up[0], w_down[0],
                  norm_final_g)
```

```python
import functools

import jax
import jax.numpy as jnp
from jax import lax
from jax.experimental import pallas as pl
from jax.experimental.pallas import tpu as pltpu

F32 = jnp.float32
BF16 = jnp.bfloat16

HEAD_DIM = 64
N_SB_HEADS = 8
N_SW_HEADS = 8
N_SW_KV_HEADS = 2
WINDOW = 128
ROPE_THETA = 10000.0
N_GROUPS = 4
EXPERTS_PER_GROUP = 4
N_PAIRS = 6
N_BUCKETS = N_GROUPS * N_PAIRS
EPS = 1e-6

LANES = 128
VMEM_LIMIT = 56 * 1024 * 1024
LOG2_E = 1.4426950408889634
SB_SKIP_LOG2 = -104.0 * LOG2_E

TM_PROJ = 512
SB_BLOCK = 256
ROW_BLOCK = 256
TM_ROUTE = 512
TM_COMB = 256
META_ROWS = 32


def _cparams(sem, vmem=VMEM_LIMIT):
    return pltpu.CompilerParams(dimension_semantics=sem, vmem_limit_bytes=vmem)


def _adaln_kernel(c_ref, w_ref, b_ref, o_ref):
    c = c_ref[...]
    s = c * jax.nn.sigmoid(c)
    o_ref[...] = jnp.dot(s, w_ref[...], precision=lax.Precision.HIGHEST,
                         preferred_element_type=F32) + b_ref[...]


def _adaln(c_pad, w, b):
    rows, d = c_pad.shape
    n = w.shape[1]
    tn = 1024
    return pl.pallas_call(
        _adaln_kernel,
        grid=(n // tn,),
        in_specs=[pl.BlockSpec((rows, d), lambda j: (0, 0)),
                  pl.BlockSpec((d, tn), lambda j: (0, j)),
                  pl.BlockSpec((1, tn), lambda j: (0, j))],
        out_specs=pl.BlockSpec((rows, tn), lambda j: (0, j)),
        out_shape=jax.ShapeDtypeStruct((rows, n), F32),
        compiler_params=_cparams(("parallel",)),
        name="adaln",
    )(c_pad, w, b)


def _rope_kernel(pos_ref, freq_ref, sign_ref, cos_ref, sin_ref):
    ang = pos_ref[...].astype(F32) * freq_ref[...]
    cos_ref[...] = jnp.cos(ang)
    sin_ref[...] = jnp.sin(ang) * sign_ref[...]


def _rope_tables(pos_b, freq, sign):
    t = pos_b.shape[0]
    tm = 1024
    row = pl.BlockSpec((tm, LANES), lambda i: (i, 0))
    vec = pl.BlockSpec((1, LANES), lambda i: (0, 0))
    return pl.pallas_call(
        _rope_kernel,
        grid=(t // tm,),
        in_specs=[row, vec, vec],
        out_specs=[row, row],
        out_shape=[jax.ShapeDtypeStruct((t, LANES), F32)] * 2,
        compiler_params=_cparams(("parallel",)),
        name="rope",
    )(pos_b, freq, sign)


def _rms(x):
    return x * lax.rsqrt(jnp.mean(x * x, axis=-1, keepdims=True) + EPS)


def _rotate_half_pairs(x):
    lane = lax.broadcasted_iota(jnp.int32, x.shape, 1)
    first_half = (lane % HEAD_DIM) < (HEAD_DIM // 2)
    return jnp.where(first_half, pltpu.roll(x, LANES - HEAD_DIM // 2, 1), pltpu.roll(x, HEAD_DIM // 2, 1))


def _inproj_kernel(d_sb3, d_swq, x_ref, mod_ref, g_ref, w_ref, cos_ref, sin_ref, sb_ref, q_ref, kv_ref):
    x = x_ref[...]
    shift = mod_ref[0, 0:1, :]
    scale = mod_ref[0, 1:2, :]
    h = (_rms(x) * g_ref[...]) * (1.0 + scale) + shift
    hb = h.astype(BF16)
    chunk = 512
    for c0 in range(0, d_sb3, chunk):
        sb_ref[:, c0:c0 + chunk] = jnp.dot(hb, w_ref[:, c0:c0 + chunk],
                                           preferred_element_type=F32).astype(BF16)
    cos = cos_ref[...]
    sin = sin_ref[...]
    q = jnp.dot(hb, w_ref[:, d_sb3:d_sb3 + d_swq], preferred_element_type=F32)
    for c0 in range(0, d_swq, LANES):
        qt = q[:, c0:c0 + LANES]
        q_ref[:, c0:c0 + LANES] = (qt * cos + _rotate_half_pairs(qt) * sin).astype(BF16)
    kv = jnp.dot(hb, w_ref[:, d_sb3 + d_swq:], preferred_element_type=F32)
    k = kv[:, :LANES]
    kv_ref[:, :LANES] = (k * cos + _rotate_half_pairs(k) * sin).astype(BF16)
    kv_ref[:, LANES:] = kv[:, LANES:].astype(BF16)


def _inproj(x2, mod3, g, w_bf, cos_t, sin_t, tiles_per_batch, d_sb3, d_swq):
    t, d = x2.shape
    d_in = w_bf.shape[1]
    d_kv = d_in - d_sb3 - d_swq
    tm = TM_PROJ
    return pl.pallas_call(
        functools.partial(_inproj_kernel, d_sb3, d_swq),
        grid=(t // tm,),
        in_specs=[pl.BlockSpec((tm, d), lambda i: (i, 0)),
                  pl.BlockSpec((1, 6, d), lambda i: (i // tiles_per_batch, 0, 0)),
                  pl.BlockSpec((1, d), lambda i: (0, 0)),
                  pl.BlockSpec((d, d_in), lambda i: (0, 0)),
                  pl.BlockSpec((tm, LANES), lambda i: (i, 0)),
                  pl.BlockSpec((tm, LANES), lambda i: (i, 0))],
        out_specs=[pl.BlockSpec((tm, d_sb3), lambda i: (i, 0)),
                   pl.BlockSpec((tm, d_swq), lambda i: (i, 0)),
                   pl.BlockSpec((tm, d_kv), lambda i: (i, 0))],
        out_shape=[jax.ShapeDtypeStruct((t, d_sb3), BF16),
                   jax.ShapeDtypeStruct((t, d_swq), BF16),
                   jax.ShapeDtypeStruct((t, d_kv), BF16)],
        compiler_params=_cparams(("parallel",)),
        name="inproj",
    )(x2, mod3, g, w_bf, cos_t, sin_t)


def _dot_nt(a, b):
    return lax.dot_general(a, b, (((1,), (1,)), ((), ())), preferred_element_type=F32)


def _sb_block(qm, k, v, tri2, carry, causal):
    z = _dot_nt(qm, k)
    log_beta = jnp.minimum(z, 0.0) - jnp.log2(1.0 + jnp.exp2(-jnp.abs(z)))
    log_rest = log_beta - z
    if causal is not None:
        log_rest = jnp.where(causal, log_rest, 0.0)
    hi = log_rest.astype(BF16)
    lo = (log_rest - hi.astype(F32)).astype(BF16)
    later = jnp.dot(jnp.concatenate([hi, lo], axis=1), tri2, preferred_element_type=F32)
    w = jnp.exp2(log_beta + later + carry)
    if causal is not None:
        w = jnp.where(causal, w, 0.0)
    out = jnp.dot(w.astype(BF16), v, preferred_element_type=F32)
    return out, carry + jnp.sum(log_rest, axis=1, keepdims=True)


def _sb_kernel(blk, q_ref, k_ref, v_ref, tri_ref, o_ref):
    i = pl.program_id(2)
    q = q_ref[0]
    tri2 = tri_ref[...]
    lane = lax.broadcasted_iota(jnp.int32, q.shape, 1)
    row = lax.broadcasted_iota(jnp.int32, (blk, blk), 0)
    col = lax.broadcasted_iota(jnp.int32, (blk, blk), 1)
    causal = col < row
    qms = [jnp.where((lane >= h * HEAD_DIM) & (lane < (h + 1) * HEAD_DIM), q, jnp.zeros_like(q)) for h in range(2)]

    def sweep(kb, carries, causal_mask):
        s = pl.multiple_of(kb * blk, blk)
        k = k_ref[0, pl.ds(s, blk), :]
        v = v_ref[0, pl.ds(s, blk), :]
        return [_sb_block(qms[h], k, v, tri2, carries[h], causal_mask) for h in range(2)]

    diag = sweep(i, [jnp.zeros((blk, 1), F32)] * 2, causal)
    left = sweep(jnp.maximum(i - 1, 0), [jnp.where(i > 0, diag[h][1], -1e30) for h in range(2)], None)

    def cond(state):
        kb, c0, c1, _, _ = state
        return jnp.logical_and(kb >= 0, jnp.max(jnp.maximum(c0, c1)) >= SB_SKIP_LOG2)

    def body(state):
        kb, c0, c1, a0, a1 = state
        (o0, c0), (o1, c1) = sweep(kb, [c0, c1], None)
        return kb - 1, c0, c1, a0 + o0, a1 + o1

    _, _, _, a0, a1 = lax.while_loop(
        cond, body, (i - 2, left[0][1], left[1][1], diag[0][0] + left[0][0], diag[1][0] + left[1][0]))
    o_ref[0] = jnp.where(lane < HEAD_DIM, a0, a1).astype(o_ref.dtype)


def _sb_attention(qkv3, tri):
    b, s, d3 = qkv3.shape
    d_sb = d3 // 3
    n_pairs = d_sb // LANES
    blk = SB_BLOCK
    return pl.pallas_call(
        functools.partial(_sb_kernel, blk),
        grid=(b, n_pairs, s // blk),
        in_specs=[pl.BlockSpec((1, blk, LANES), lambda bi, p, i: (bi, i, p)),
                  pl.BlockSpec((1, s, LANES), lambda bi, p, i: (bi, 0, n_pairs + p)),
                  pl.BlockSpec((1, s, LANES), lambda bi, p, i: (bi, 0, 2 * n_pairs + p)),
                  pl.BlockSpec((2 * blk, blk), lambda bi, p, i: (0, 0))],
        out_specs=pl.BlockSpec((1, blk, LANES), lambda bi, p, i: (bi, i, p)),
        out_shape=jax.ShapeDtypeStruct((b, s, d_sb), BF16),
        compiler_params=_cparams(("parallel", "parallel", "arbitrary")),
        name="sb_attn",
    )(qkv3, qkv3, qkv3, tri)


def _sw_kernel(n_tiles, sinks_ref, q_ref, kvp_ref, kvc_ref, o_ref):
    i = pl.program_id(1)
    w = WINDOW
    kk = jnp.concatenate([kvp_ref[0, :, :LANES], kvc_ref[0, :, :LANES]], axis=0)
    vv = jnp.concatenate([kvp_ref[0, :, LANES:], kvc_ref[0, :, LANES:]], axis=0)
    qi = lax.broadcasted_iota(jnp.int32, (w, 2 * w), 0)
    kj = lax.broadcasted_iota(jnp.int32, (w, 2 * w), 1)
    diff = w + qi - kj
    mask = (diff >= 0) & (diff < w) & (i * w + kj - w >= 0)
    lane = lax.broadcasted_iota(jnp.int32, (w, LANES), 1)
    for t in range(n_tiles):
        qt = q_ref[0, :, t * LANES:(t + 1) * LANES]
        outs = []
        for half in range(2):
            head = t + half * n_tiles
            in_head = (lane >= half * HEAD_DIM) & (lane < (half + 1) * HEAD_DIM)
            qm = jnp.where(in_head, qt, jnp.zeros_like(qt))
            s = jnp.where(mask, _dot_nt(qm, kk), -1e30)
            sink = sinks_ref[head]
            m = jnp.maximum(jnp.max(s, axis=-1, keepdims=True), sink)
            p = jnp.exp(s - m)
            denom = jnp.sum(p, axis=-1, keepdims=True) + jnp.exp(sink - m)
            outs.append(jnp.dot(p.astype(BF16), vv, preferred_element_type=F32) * (1.0 / denom))
        o_ref[0, :, t * LANES:(t + 1) * LANES] = jnp.where(lane < HEAD_DIM, outs[0], outs[1]).astype(o_ref.dtype)


def _sw_attention(sinks, q3, kv3):
    b, s, dq = q3.shape
    dkv = kv3.shape[2]
    w = WINDOW
    n_tiles = dq // LANES
    return pl.pallas_call(
        functools.partial(_sw_kernel, n_tiles),
        grid=(b, s // w),
        in_specs=[pl.BlockSpec(memory_space=pltpu.SMEM),
                  pl.BlockSpec((1, w, dq), lambda bi, i: (bi, i, 0)),
                  pl.BlockSpec((1, w, dkv), lambda bi, i: (bi, jnp.maximum(i - 1, 0), 0)),
                  pl.BlockSpec((1, w, dkv), lambda bi, i: (bi, i, 0))],
        out_specs=pl.BlockSpec((1, w, dq), lambda bi, i: (bi, i, 0)),
        out_shape=jax.ShapeDtypeStruct((b, s, dq), BF16),
        compiler_params=_cparams(("parallel", "parallel")),
        name="sw_attn",
    )(sinks, q3, kv3, kv3)


def _first_max(vals):
    m = vals[0]
    for v in vals[1:]:
        m = jnp.maximum(m, v)
    idx = jnp.full(m.shape, len(vals) - 1, jnp.int32)
    for k in range(len(vals) - 2, -1, -1):
        idx = jnp.where(vals[k] == m, k, idx)
    return m, idx


def _route(lg):
    g = [lg[k:k + 1, :] for k in range(N_GROUPS)]
    gmax, gsel = _first_max(g)
    gsum = g[0] * 0.0
    for k in range(N_GROUPS):
        gsum = gsum + jnp.exp(g[k] - gmax)
    gprob = 1.0 / gsum
    within = []
    for k in range(EXPERTS_PER_GROUP):
        v = lg[N_GROUPS + (N_GROUPS - 1) * EXPERTS_PER_GROUP + k:N_GROUPS + (N_GROUPS - 1) * EXPERTS_PER_GROUP + k + 1, :]
        for grp in range(N_GROUPS - 2, -1, -1):
            r = N_GROUPS + grp * EXPERTS_PER_GROUP + k
            v = jnp.where(gsel == grp, lg[r:r + 1, :], v)
        within.append(v)
    m1, i1 = _first_max(within)
    rest = [jnp.where(i1 == k, -jnp.inf, within[k]) for k in range(EXPERTS_PER_GROUP)]
    m2, i2 = _first_max(rest)
    e2 = jnp.exp(m2 - m1)
    p1 = gprob / (1.0 + e2)
    p2 = gprob * e2 / (1.0 + e2)
    lo = jnp.minimum(i1, i2)
    hi = jnp.maximum(i1, i2)
    w_lo = jnp.where(i1 < i2, p1, p2)
    w_hi = jnp.where(i1 < i2, p2, p1)
    pair = jnp.where(lo == 0, hi - 1, jnp.where(lo == 1, hi + 1, 5))
    return gsel * N_PAIRS + pair, w_lo, w_hi


def _ffn_input(x1, mod_ref, gffn_ref):
    shift2 = mod_ref[0, 3:4, :]
    scale2 = mod_ref[0, 4:5, :]
    return (_rms(x1) * gffn_ref[...]) * (1.0 + scale2) + shift2


def _outproj_kernel(d_sb, osb_ref, osw_ref, x_ref, mod_ref, gsb_ref, gsw_ref, wo_ref, gffn_ref, wr_ref, tri_ref,
                    x1_ref, meta_ref, wts_ref, cnt_ref, run_ref):
    i = pl.program_id(0)
    tm = x_ref.shape[0]

    @pl.when(i == 0)
    def _():
        run_ref[...] = jnp.zeros_like(run_ref)

    gate1 = mod_ref[0, 2:3, :]
    a = (_rms(osb_ref[...].astype(F32)) * gsb_ref[...]).astype(BF16)
    bsw = (_rms(osw_ref[...].astype(F32)) * gsw_ref[...]).astype(BF16)
    mix = jnp.dot(a, wo_ref[:d_sb, :], preferred_element_type=F32) + jnp.dot(bsw, wo_ref[d_sb:, :],
                                                                                preferred_element_type=F32)
    x1 = x_ref[...] + gate1 * mix
    x1_ref[...] = x1
    h2 = _ffn_input(x1, mod_ref, gffn_ref)
    lg = lax.dot_general(wr_ref[...], h2, (((1,), (1,)), ((), ())), precision=lax.Precision.HIGHEST,
                         preferred_element_type=F32)
    bucket, w_lo, w_hi = _route(lg)
    rows = lax.broadcasted_iota(jnp.int32, (META_ROWS, tm), 0)
    onehot = rows == bucket
    prefix = jnp.dot(jnp.where(onehot, 1.0, 0.0).astype(BF16), tri_ref[...], preferred_element_type=F32)
    run = run_ref[...]
    rank = jnp.sum(jnp.where(onehot, prefix - 1.0 + run, 0.0), axis=0, keepdims=True)
    run = run + jnp.sum(jnp.where(onehot, 1.0, 0.0), axis=1, keepdims=True)
    run_ref[...] = run
    cnt_ref[...] = jnp.broadcast_to(run, cnt_ref.shape)
    meta_ref[...] = jnp.concatenate([bucket, rank.astype(jnp.int32), jnp.zeros((6, tm), jnp.int32)], axis=0)
    wts_ref[...] = jnp.concatenate([w_lo, w_hi, jnp.zeros((6, tm), F32)], axis=0)


def _outproj(o_sb, o_sw, x2, mod3, g_sb, g_sw, wo_bf, g_ffn, wr, tri, tiles_per_batch):
    t, d = x2.shape
    d_sb = o_sb.shape[1]
    d_sw = o_sw.shape[1]
    tm = TM_ROUTE
    const = lambda shape: pl.BlockSpec(shape, lambda i: tuple(0 for _ in shape))
    return pl.pallas_call(
        functools.partial(_outproj_kernel, d_sb),
        grid=(t // tm,),
        in_specs=[pl.BlockSpec((tm, d_sb), lambda i: (i, 0)),
                  pl.BlockSpec((tm, d_sw), lambda i: (i, 0)),
                  pl.BlockSpec((tm, d), lambda i: (i, 0)),
                  pl.BlockSpec((1, 6, d), lambda i: (i // tiles_per_batch, 0, 0)),
                  const((1, d_sb)), const((1, d_sw)), const((d_sb + d_sw, d)), const((1, d)),
                  const((META_ROWS, d)), const((tm, tm))],
        out_specs=[pl.BlockSpec((tm, d), lambda i: (i, 0)),
                   pl.BlockSpec((8, tm), lambda i: (0, i)),
                   pl.BlockSpec((8, tm), lambda i: (0, i)),
                   const((META_ROWS, LANES))],
        out_shape=[jax.ShapeDtypeStruct((t, d), F32),
                   jax.ShapeDtypeStruct((8, t), jnp.int32),
                   jax.ShapeDtypeStruct((8, t), F32),
                   jax.ShapeDtypeStruct((META_ROWS, LANES), F32)],
        scratch_shapes=[pltpu.VMEM((META_ROWS, 1), F32)],
        compiler_params=_cparams(("arbitrary",)),
        name="outproj",
    )(o_sb, o_sw, x2, mod3, g_sb, g_sw, wo_bf, g_ffn, wr, tri)


ROW_TILE = 8


def _tile_copy(src_ref, src_row, dst_ref, dst_row, sem):
    src = src_ref.at[pl.ds(pl.multiple_of(src_row * ROW_TILE, ROW_TILE), ROW_TILE), :]
    dst = dst_ref.at[pl.ds(pl.multiple_of(dst_row * ROW_TILE, ROW_TILE), ROW_TILE), :]
    return pltpu.make_async_copy(src, dst, sem)


def _wait_tiles(vmem_ref, hbm_ref, sem):
    n = vmem_ref.shape[0]
    pltpu.make_async_copy(hbm_ref.at[pl.ds(0, n), :], vmem_ref, sem).wait()


def _dispatch_kernel(dest_ref, x1_ref, mod_ref, gffn_ref, wts_ref, zeros_ref, out_ref, rows, sems):
    del zeros_ref
    i = pl.program_id(0)
    n = pl.num_programs(0)
    tm, d = x1_ref.shape
    words_per_row = d // 2
    slot = i % 2
    buf = rows.at[slot]

    @pl.when(i >= 2)
    def _():
        _wait_tiles(buf, out_ref, sems.at[slot])

    h2 = _ffn_input(x1_ref[...], mod_ref, gffn_ref)
    hi = pltpu.bitcast(h2[:, :words_per_row].astype(BF16).astype(F32), jnp.uint32)
    lo = pltpu.bitcast(h2[:, words_per_row:].astype(BF16).astype(F32), jnp.uint32)
    words = hi | (lo >> 16)
    n_word_rows = words_per_row // LANES
    for s in range(n_word_rows):
        buf[pl.ds(s, tm, stride=ROW_TILE), :] = words[:, s * LANES:(s + 1) * LANES]
    wts = jnp.concatenate([wts_ref[...], jnp.zeros((LANES - 8, tm), F32)], axis=0).T
    buf[pl.ds(n_word_rows, tm, stride=ROW_TILE), :] = pltpu.bitcast(wts, jnp.uint32)
    for s in range(n_word_rows + 1, ROW_TILE):
        buf[pl.ds(s, tm, stride=ROW_TILE), :] = jnp.zeros((tm, LANES), jnp.uint32)

    def issue(r, c):
        _tile_copy(buf, r, out_ref, dest_ref[i * tm + r], sems.at[slot]).start()
        return c

    lax.fori_loop(0, tm, issue, 0, unroll=8)

    @pl.when(i == n - 1)
    def _():
        _wait_tiles(buf, out_ref, sems.at[slot])

        @pl.when(n >= 2)
        def _():
            _wait_tiles(rows.at[1 - slot], out_ref, sems.at[1 - slot])


def _dispatch(dest, x1, mod3, g_ffn, wts, zeros_sorted, tiles_per_batch):
    t, d = x1.shape
    tm = TM_ROUTE
    return pl.pallas_call(
        _dispatch_kernel,
        grid_spec=pltpu.PrefetchScalarGridSpec(
            num_scalar_prefetch=1,
            grid=(t // tm,),
            in_specs=[pl.BlockSpec((tm, d), lambda i, dest: (i, 0)),
                      pl.BlockSpec((1, 6, d), lambda i, dest: (i // tiles_per_batch, 0, 0)),
                      pl.BlockSpec((1, d), lambda i, dest: (0, 0)),
                      pl.BlockSpec((8, tm), lambda i, dest: (0, i)),
                      pl.BlockSpec(memory_space=pl.ANY)],
            out_specs=pl.BlockSpec(memory_space=pl.ANY),
            scratch_shapes=[pltpu.VMEM((2, tm * ROW_TILE, LANES), jnp.uint32), pltpu.SemaphoreType.DMA((2,))]),
        out_shape=jax.ShapeDtypeStruct(zeros_sorted.shape, zeros_sorted.dtype),
        input_output_aliases={5: 0},
        compiler_params=_cparams(("arbitrary",)),
        name="dispatch",
    )(dest, x1, mod3, g_ffn, wts, zeros_sorted)


def _experts_kernel(d, ea_ref, eb_ref, nvalid_ref, x_ref, wga_ref, wua_ref, wda_ref, wgb_ref, wub_ref, wdb_ref, y_ref):
    j = pl.program_id(0)
    rb = x_ref.shape[0] // ROW_TILE
    n_word_rows = d // 2 // LANES

    @pl.when(j < nvalid_ref[0])
    def _():
        high, low = [], []
        for s in range(n_word_rows):
            w = x_ref[pl.ds(s, rb, stride=ROW_TILE), :]
            high.append(pltpu.bitcast(w & jnp.uint32(0xFFFF0000), F32).astype(BF16))
            low.append(pltpu.bitcast(w << 16, F32).astype(BF16))
        xb = jnp.concatenate(high + low, axis=1)
        wts = pltpu.bitcast(x_ref[pl.ds(n_word_rows, rb, stride=ROW_TILE), :], F32)

        def hidden(wg_ref, wu_ref, wrow):
            gt = jnp.dot(xb, wg_ref[0], preferred_element_type=F32)
            up = jnp.dot(xb, wu_ref[0], preferred_element_type=F32)
            return ((gt * jax.nn.sigmoid(gt)) * up * wrow).astype(BF16)

        ha = hidden(wga_ref, wua_ref, wts[:, 0:1])
        hb = hidden(wgb_ref, wub_ref, wts[:, 1:2])
        y = (jnp.dot(ha, wda_ref[0], preferred_element_type=F32)
             + jnp.dot(hb, wdb_ref[0], preferred_element_type=F32))
        for c in range(ROW_TILE):
            y_ref[pl.ds(c, rb, stride=ROW_TILE), :] = y[:, c * LANES:(c + 1) * LANES]

    @pl.when(j >= nvalid_ref[0])
    def _():
        y_ref[...] = jnp.zeros_like(y_ref)


def _experts(blk_ea, blk_eb, nvalid, sorted_rows, wg_bf, wu_bf, wd_bf):
    n_e, d, d_e = wg_bf.shape
    assert d == ROW_TILE * LANES, "one expert output row must fill one (8, 128) f32 tile"
    rb = ROW_BLOCK
    nb = sorted_rows.shape[0] // (rb * ROW_TILE)
    up_a = pl.BlockSpec((1, d, d_e), lambda j, ea, eb, nv: (ea[j], 0, 0))
    up_b = pl.BlockSpec((1, d, d_e), lambda j, ea, eb, nv: (eb[j], 0, 0))
    dn_a = pl.BlockSpec((1, d_e, d), lambda j, ea, eb, nv: (ea[j], 0, 0))
    dn_b = pl.BlockSpec((1, d_e, d), lambda j, ea, eb, nv: (eb[j], 0, 0))
    rows = pl.BlockSpec((rb * ROW_TILE, LANES), lambda j, ea, eb, nv: (j, 0))
    return pl.pallas_call(
        functools.partial(_experts_kernel, d),
        grid_spec=pltpu.PrefetchScalarGridSpec(
            num_scalar_prefetch=3,
            grid=(nb,),
            in_specs=[rows, up_a, up_a, dn_a, up_b, up_b, dn_b],
            out_specs=rows),
        out_shape=jax.ShapeDtypeStruct(sorted_rows.shape, F32),
        compiler_params=_cparams(("arbitrary",)),
        name="experts",
    )(blk_ea, blk_eb, nvalid, sorted_rows, wg_bf, wu_bf, wd_bf, wg_bf, wu_bf, wd_bf)


def _combine_kernel(dest_ref, x1_ref, mod_ref, g_ref, y_hbm, o_ref, ybuf, sems):
    i = pl.program_id(0)
    n = pl.num_programs(0)
    tm = x1_ref.shape[0]
    slot = i % 2

    def issue(tile, slot_):
        def one(r, c):
            _tile_copy(y_hbm, dest_ref[tile * tm + r], ybuf.at[slot_], r, sems.at[slot_]).start()
            return c
        lax.fori_loop(0, tm, one, 0, unroll=8)

    @pl.when(i == 0)
    def _():
        issue(0, 0)

    @pl.when(i + 1 < n)
    def _():
        issue(i + 1, 1 - slot)

    _wait_tiles(ybuf.at[slot], y_hbm, sems.at[slot])
    y = jnp.concatenate([ybuf[slot, pl.ds(c, tm, stride=ROW_TILE), :] for c in range(ROW_TILE)], axis=1)
    gate2 = mod_ref[0, 5:6, :]
    o_ref[...] = _rms(x1_ref[...] + gate2 * y) * g_ref[...]


def _combine(dest, x1, mod3, g_final, y_sorted, tiles_per_batch):
    t, d = x1.shape
    tm = TM_COMB
    return pl.pallas_call(
        _combine_kernel,
        grid_spec=pltpu.PrefetchScalarGridSpec(
            num_scalar_prefetch=1,
            grid=(t // tm,),
            in_specs=[pl.BlockSpec((tm, d), lambda i, dest: (i, 0)),
                      pl.BlockSpec((1, 6, d), lambda i, dest: (i // tiles_per_batch, 0, 0)),
                      pl.BlockSpec((1, d), lambda i, dest: (0, 0)),
                      pl.BlockSpec(memory_space=pl.ANY)],
            out_specs=pl.BlockSpec((tm, d), lambda i, dest: (i, 0)),
            scratch_shapes=[pltpu.VMEM((2, tm * ROW_TILE, LANES), F32), pltpu.SemaphoreType.DMA((2,))]),
        out_shape=jax.ShapeDtypeStruct((t, d), F32),
        compiler_params=_cparams(("arbitrary",)),
        name="combine",
    )(dest, x1, mod3, g_final, y_sorted)


def _pair_table():
    lo, hi = [], []
    for g in range(N_GROUPS):
        for a in range(EXPERTS_PER_GROUP):
            for b in range(a + 1, EXPERTS_PER_GROUP):
                lo.append(g * EXPERTS_PER_GROUP + a)
                hi.append(g * EXPERTS_PER_GROUP + b)
    return jnp.array(lo, jnp.int32), jnp.array(hi, jnp.int32)


def _layer(x, mod3, positions, norm_mix_g, w_in, sinks, out_norm_sb_g, out_norm_sw_g, w_out, norm_ffn_g,
           w_router_group, w_router_expert, w_gate, w_up, w_down, norm_final_g):
    b, s, d = x.shape
    t = b * s
    d_sb = N_SB_HEADS * HEAD_DIM
    d_sw = N_SW_HEADS * HEAD_DIM
    d_kv = N_SW_KV_HEADS * HEAD_DIM
    scale = HEAD_DIM ** -0.5
    x2 = x.reshape(t, d)

    n_tiles = d_sw // LANES
    sw_heads = jnp.array([h for j in range(n_tiles) for h in (j, j + n_tiles)], jnp.int32)
    sw_cols = (sw_heads[:, None] * HEAD_DIM + jnp.arange(HEAD_DIM, dtype=jnp.int32)[None, :]).reshape(-1)
    q_sw_cols = 3 * d_sb + sw_cols
    w_bf = jnp.concatenate([w_in[:, :d_sb] * (scale * LOG2_E), w_in[:, d_sb:3 * d_sb],
                            w_in[:, q_sw_cols] * scale, w_in[:, 3 * d_sb + d_sw:]], axis=1).astype(BF16)

    inv_freq = ROPE_THETA ** (-jnp.arange(0, HEAD_DIM, 2, dtype=F32) / HEAD_DIM)
    freq = jnp.tile(inv_freq, LANES // (HEAD_DIM // 2)).reshape(1, LANES)
    sign = jnp.tile(jnp.concatenate([-jnp.ones(HEAD_DIM // 2, F32), jnp.ones(HEAD_DIM // 2, F32)]),
                    LANES // HEAD_DIM).reshape(1, LANES)
    pos_b = jnp.broadcast_to(positions.reshape(t, 1), (t, LANES))
    cos_t, sin_t = _rope_tables(pos_b, freq, sign)

    qkv_sb, q_sw, kv_sw = _inproj(x2, mod3, norm_mix_g.reshape(1, d), w_bf, cos_t, sin_t, s // TM_PROJ,
                                  3 * d_sb, d_sw)

    tri_sb = (jnp.arange(SB_BLOCK)[:, None] > jnp.arange(SB_BLOCK)[None, :]).astype(BF16)
    tri_sb = jnp.concatenate([tri_sb, tri_sb], axis=0)
    o_sb = _sb_attention(qkv_sb.reshape(b, s, 3 * d_sb), tri_sb).reshape(t, d_sb)
    o_sw = _sw_attention(sinks, q_sw.reshape(b, s, d_sw), kv_sw.reshape(b, s, 2 * d_kv)).reshape(t, d_sw)

    wo_bf = jnp.concatenate([w_out[:d_sb], w_out[d_sb + sw_cols]], axis=0).astype(BF16)
    wr = jnp.concatenate([w_router_group.T, w_router_expert.T,
                          jnp.zeros((META_ROWS - N_GROUPS - N_GROUPS * EXPERTS_PER_GROUP, d), F32)], axis=0)
    tri_rt = (jnp.arange(TM_ROUTE)[:, None] <= jnp.arange(TM_ROUTE)[None, :]).astype(BF16)
    g_ffn = norm_ffn_g.reshape(1, d)
    x1, meta, wts, cnt = _outproj(o_sb, o_sw, x2, mod3, out_norm_sb_g.reshape(1, d_sb),
                                  out_norm_sw_g[sw_cols].reshape(1, d_sw), wo_bf, g_ffn, wr, tri_rt, s // TM_ROUTE)

    counts = cnt[:N_BUCKETS, 0].astype(jnp.int32)
    padded = (counts + ROW_BLOCK - 1) // ROW_BLOCK * ROW_BLOCK
    pend = jnp.cumsum(padded)
    pstart = pend - padded
    dest = pstart[meta[0]] + meta[1]
    p_rows = t + N_BUCKETS * ROW_BLOCK
    nb = p_rows // ROW_BLOCK
    blk_start = jnp.arange(nb, dtype=jnp.int32) * ROW_BLOCK
    blk_bucket = jnp.minimum(jnp.sum((pend[None, :] <= blk_start[:, None]).astype(jnp.int32), axis=1), N_BUCKETS - 1)
    e_lo, e_hi = _pair_table()
    nvalid = (pend[-1:] // ROW_BLOCK).astype(jnp.int32)

    sorted_rows = _dispatch(dest, x1, mod3, g_ffn, wts, jnp.zeros((p_rows * ROW_TILE, LANES), jnp.uint32),
                            s // TM_ROUTE)
    y_sorted = _experts(e_lo[blk_bucket], e_hi[blk_bucket], nvalid, sorted_rows,
                        w_gate.astype(BF16), w_up.astype(BF16), w_down.astype(BF16))
    return _combine(dest, x1, mod3, norm_final_g.reshape(1, d), y_sorted, s // TM_COMB).reshape(b, s, d)


def kernel(x, c, positions, w_ada, b_ada, norm_mix_g, w_in, sinks, out_norm_sb_g, out_norm_sw_g, w_out, norm_ffn_g,
           w_router_group, w_router_expert, w_gate, w_up, w_down, norm_final_g):
    b, s, d = x.shape
    depth = w_ada.shape[0]
    assert depth == 1, "the final norm is fused into the layer's last stage"
    c_pad = jnp.pad(c, ((0, 8 - b), (0, 0)))
    mod = _adaln(c_pad, w_ada[0], b_ada[0].reshape(1, -1))
    mod3 = mod[:b].reshape(b, 6, d)
    return _layer(x, mod3, positions, norm_mix_g[0], w_in[0], sinks[0], out_norm_sb_g[0], out_norm_sw_g[0],
                  w_out[0], norm_ffn_g[0], w_router_group[0], w_router_expert[0], w_gate[0], w_up[0], w_down[0],
                  norm_final_g)
```

```python
import functools

import jax
import jax.numpy as jnp
from jax import lax
from jax.experimental import pallas as pl
from jax.experimental.pallas import tpu as pltpu

F32 = jnp.float32
BF16 = jnp.bfloat16

HEAD_DIM = 64
N_SB_HEADS = 8
N_SW_HEADS = 8
N_SW_KV_HEADS = 2
WINDOW = 128
ROPE_THETA = 10000.0
N_GROUPS = 4
EXPERTS_PER_GROUP = 4
N_PAIRS = 6
N_BUCKETS = N_GROUPS * N_PAIRS
EPS = 1e-6

LANES = 128
VMEM_LIMIT = 56 * 1024 * 1024
LOG2_E = 1.4426950408889634
SB_SKIP_LOG2 = -104.0 * LOG2_E

TM_PROJ = 512
SB_BLOCK = 256
ROW_BLOCK = 256
TM_ROUTE = 512
TM_COMB = 512
SB_BLOCKS_PER_STEP = 4
SW_BLOCKS_PER_STEP = 4
META_ROWS = 32


def _cparams(sem, vmem=VMEM_LIMIT):
    return pltpu.CompilerParams(dimension_semantics=sem, vmem_limit_bytes=vmem)


def _adaln_kernel(c_ref, w_ref, b_ref, o_ref):
    c = c_ref[...]
    s = c * jax.nn.sigmoid(c)
    o_ref[...] = jnp.dot(s, w_ref[...], precision=lax.Precision.HIGHEST,
                         preferred_element_type=F32) + b_ref[...]


def _adaln(c_pad, w, b):
    rows, d = c_pad.shape
    n = w.shape[1]
    tn = 1024
    return pl.pallas_call(
        _adaln_kernel,
        grid=(n // tn,),
        in_specs=[pl.BlockSpec((rows, d), lambda j: (0, 0)),
                  pl.BlockSpec((d, tn), lambda j: (0, j)),
                  pl.BlockSpec((1, tn), lambda j: (0, j))],
        out_specs=pl.BlockSpec((rows, tn), lambda j: (0, j)),
        out_shape=jax.ShapeDtypeStruct((rows, n), F32),
        compiler_params=_cparams(("parallel",)),
        name="adaln",
    )(c_pad, w, b)


def _rope_kernel(pos_ref, freq_ref, cos_ref, sin_ref):
    ang = pos_ref[...].astype(F32) * freq_ref[...]
    cos_ref[...] = jnp.cos(ang)
    sin_ref[...] = jnp.sin(ang)


def _rope_tables(pos_dense, freq):
    rows = pos_dense.shape[0]
    tm = min(1024, rows)
    row = pl.BlockSpec((tm, LANES), lambda i: (i, 0))
    vec = pl.BlockSpec((1, LANES), lambda i: (0, 0))
    return pl.pallas_call(
        _rope_kernel,
        grid=(rows // tm,),
        in_specs=[row, vec],
        out_specs=[row, row],
        out_shape=[jax.ShapeDtypeStruct((rows, LANES), F32)] * 2,
        compiler_params=_cparams(("parallel",)),
        name="rope",
    )(pos_dense, freq)


def _rms(x):
    return x * lax.rsqrt(jnp.mean(x * x, axis=-1, keepdims=True) + EPS)


def _rotate_half_pairs(x):
    lane = lax.broadcasted_iota(jnp.int32, x.shape, 1)
    first_half = (lane % HEAD_DIM) < (HEAD_DIM // 2)
    return jnp.where(first_half, pltpu.roll(x, LANES - HEAD_DIM // 2, 1), pltpu.roll(x, HEAD_DIM // 2, 1))


def _inproj_kernel(d_sb3, d_swq, x_ref, mod_ref, g_ref, w_ref, cos_ref, sin_ref, sb_ref, q_ref, kv_ref):
    x = x_ref[...]
    shift = mod_ref[0, 0:1, :]
    scale = mod_ref[0, 1:2, :]
    h = (_rms(x) * g_ref[...]) * (1.0 + scale) + shift
    hb = h.astype(BF16)
    chunk = 512
    for c0 in range(0, d_sb3, chunk):
        sb_ref[:, c0:c0 + chunk] = jnp.dot(hb, w_ref[:, c0:c0 + chunk],
                                           preferred_element_type=F32).astype(BF16)
    cos = cos_ref[...]
    sin = sin_ref[...]
    q = jnp.dot(hb, w_ref[:, d_sb3:d_sb3 + d_swq], preferred_element_type=F32)
    for c0 in range(0, d_swq, LANES):
        qt = q[:, c0:c0 + LANES]
        q_ref[:, c0:c0 + LANES] = (qt * cos + _rotate_half_pairs(qt) * sin).astype(BF16)
    kv = jnp.dot(hb, w_ref[:, d_sb3 + d_swq:], preferred_element_type=F32)
    k = kv[:, :LANES]
    kv_ref[:, :LANES] = (k * cos + _rotate_half_pairs(k) * sin).astype(BF16)
    kv_ref[:, LANES:] = kv[:, LANES:].astype(BF16)


def _inproj(x2, mod3, g, w_bf, cos_t, sin_t, tiles_per_batch, d_sb3, d_swq):
    t, d = x2.shape
    d_in = w_bf.shape[1]
    d_kv = d_in - d_sb3 - d_swq
    tm = TM_PROJ
    return pl.pallas_call(
        functools.partial(_inproj_kernel, d_sb3, d_swq),
        grid=(t // tm,),
        in_specs=[pl.BlockSpec((tm, d), lambda i: (i, 0)),
                  pl.BlockSpec((1, 6, d), lambda i: (i // tiles_per_batch, 0, 0)),
                  pl.BlockSpec((1, d), lambda i: (0, 0)),
                  pl.BlockSpec((d, d_in), lambda i: (0, 0)),
                  pl.BlockSpec((tm, LANES), lambda i: (i, 0)),
                  pl.BlockSpec((tm, LANES), lambda i: (i, 0))],
        out_specs=[pl.BlockSpec((tm, d_sb3), lambda i: (i, 0)),
                   pl.BlockSpec((tm, d_swq), lambda i: (i, 0)),
                   pl.BlockSpec((tm, d_kv), lambda i: (i, 0))],
        out_shape=[jax.ShapeDtypeStruct((t, d_sb3), BF16),
                   jax.ShapeDtypeStruct((t, d_swq), BF16),
                   jax.ShapeDtypeStruct((t, d_kv), BF16)],
        compiler_params=_cparams(("parallel",)),
        name="inproj",
    )(x2, mod3, g, w_bf, cos_t, sin_t)


def _dot_nt(a, b):
    return lax.dot_general(a, b, (((1,), (1,)), ((), ())), preferred_element_type=F32)


def _sb_block(qm, k, v, tri, carry, causal):
    z = _dot_nt(qm, k)
    log_beta = jnp.minimum(z, 0.0) - jnp.log2(1.0 + jnp.exp2(-jnp.abs(z)))
    log_rest = log_beta - z
    if causal is not None:
        log_rest = jnp.where(causal, log_rest, 0.0)
    later = jnp.dot(log_rest.astype(BF16), tri, preferred_element_type=F32)
    w = jnp.exp2(log_beta + later + carry)
    if causal is not None:
        w = jnp.where(causal, w, 0.0)
    out = jnp.dot(w.astype(BF16), v, preferred_element_type=F32)
    return out, carry + jnp.sum(log_rest, axis=1, keepdims=True)


def _sb_kernel(blk, n_q, q_ref, k_ref, v_ref, tri_ref, o_ref):
    i = pl.program_id(2)
    tri = tri_ref[...]
    lane = lax.broadcasted_iota(jnp.int32, (blk, LANES), 1)
    row = lax.broadcasted_iota(jnp.int32, (2 * blk, blk), 0)
    col = lax.broadcasted_iota(jnp.int32, (2 * blk, blk), 1)
    causal = col < (row % blk)

    def sweep(qs, kb, carry, causal_mask):
        s = pl.multiple_of(kb * blk, blk)
        return _sb_block(qs, k_ref[0, pl.ds(s, blk), :], v_ref[0, pl.ds(s, blk), :], tri, carry, causal_mask)

    dense = []
    for j in range(n_q):
        g = i * n_q + j
        q = q_ref[0, j * blk:(j + 1) * blk, :]
        qs = jnp.concatenate([jnp.where((lane >= h * HEAD_DIM) & (lane < (h + 1) * HEAD_DIM), q, jnp.zeros_like(q))
                              for h in range(2)], axis=0)
        acc_d, carry_d = sweep(qs, g, jnp.zeros((2 * blk, 1), F32), causal)
        acc_l, carry_l = sweep(qs, jnp.maximum(g - 1, 0), jnp.where(g > 0, carry_d, -1e30), None)
        dense.append((g, qs, carry_l, acc_d + acc_l))

    for j, (g, qs, carry, acc) in enumerate(dense):
        def cond(state):
            kb, carry, _ = state
            return jnp.logical_and(kb >= 0, jnp.max(carry) >= SB_SKIP_LOG2)

        def body(state, qs=qs):
            kb, carry, acc = state
            out, carry = sweep(qs, kb, carry, None)
            return kb - 1, carry, acc + out

        _, _, acc = lax.while_loop(cond, body, (g - 2, carry, acc))
        o_ref[0, j * blk:(j + 1) * blk, :] = jnp.where(lane < HEAD_DIM, acc[:blk], acc[blk:]).astype(o_ref.dtype)


def _sb_attention(qkv3, tri):
    b, s, d3 = qkv3.shape
    d_sb = d3 // 3
    n_pairs = d_sb // LANES
    blk = SB_BLOCK
    n_q = SB_BLOCKS_PER_STEP
    return pl.pallas_call(
        functools.partial(_sb_kernel, blk, n_q),
        grid=(b, n_pairs, s // (blk * n_q)),
        in_specs=[pl.BlockSpec((1, blk * n_q, LANES), lambda bi, p, i: (bi, i, p)),
                  pl.BlockSpec((1, s, LANES), lambda bi, p, i: (bi, 0, n_pairs + p)),
                  pl.BlockSpec((1, s, LANES), lambda bi, p, i: (bi, 0, 2 * n_pairs + p)),
                  pl.BlockSpec((blk, blk), lambda bi, p, i: (0, 0))],
        out_specs=pl.BlockSpec((1, blk * n_q, LANES), lambda bi, p, i: (bi, i, p)),
        out_shape=jax.ShapeDtypeStruct((b, s, d_sb), BF16),
        compiler_params=_cparams(("parallel", "parallel", "arbitrary")),
        name="sb_attn",
    )(qkv3, qkv3, qkv3, tri)


def _sw_kernel(n_tiles, n_sub, sinks_ref, q_ref, kvp_ref, kvc_ref, o_ref):
    i = pl.program_id(1)
    w = WINDOW
    qi = lax.broadcasted_iota(jnp.int32, (w, 2 * w), 0)
    kj = lax.broadcasted_iota(jnp.int32, (w, 2 * w), 1)
    diff = w + qi - kj
    in_window = (diff >= 0) & (diff < w)
    lane = lax.broadcasted_iota(jnp.int32, (w, LANES), 1)
    for j in range(n_sub):
        if j == 0:
            kk = jnp.concatenate([kvp_ref[0, :, :LANES], kvc_ref[0, :w, :LANES]], axis=0)
            vv = jnp.concatenate([kvp_ref[0, :, LANES:], kvc_ref[0, :w, LANES:]], axis=0)
            mask = in_window & ((i * n_sub) * w + kj - w >= 0)
        else:
            kk = kvc_ref[0, (j - 1) * w:(j + 1) * w, :LANES]
            vv = kvc_ref[0, (j - 1) * w:(j + 1) * w, LANES:]
            mask = in_window
        for t in range(n_tiles):
            qt = q_ref[0, j * w:(j + 1) * w, t * LANES:(t + 1) * LANES]
            outs = []
            for half in range(2):
                head = t + half * n_tiles
                in_head = (lane >= half * HEAD_DIM) & (lane < (half + 1) * HEAD_DIM)
                qm = jnp.where(in_head, qt, jnp.zeros_like(qt))
                s = jnp.where(mask, _dot_nt(qm, kk), -1e30)
                sink = sinks_ref[head]
                m = jnp.maximum(jnp.max(s, axis=-1, keepdims=True), sink)
                p = jnp.exp(s - m)
                denom = jnp.sum(p, axis=-1, keepdims=True) + jnp.exp(sink - m)
                outs.append(jnp.dot(p.astype(BF16), vv, preferred_element_type=F32) * (1.0 / denom))
            o_ref[0, j * w:(j + 1) * w, t * LANES:(t + 1) * LANES] = jnp.where(
                lane < HEAD_DIM, outs[0], outs[1]).astype(o_ref.dtype)


def _sw_attention(sinks, q3, kv3):
    b, s, dq = q3.shape
    dkv = kv3.shape[2]
    w = WINDOW
    n_sub = SW_BLOCKS_PER_STEP
    n_tiles = dq // LANES
    return pl.pallas_call(
        functools.partial(_sw_kernel, n_tiles, n_sub),
        grid=(b, s // (w * n_sub)),
        in_specs=[pl.BlockSpec(memory_space=pltpu.SMEM),
                  pl.BlockSpec((1, w * n_sub, dq), lambda bi, i: (bi, i, 0)),
                  pl.BlockSpec((1, w, dkv), lambda bi, i: (bi, jnp.maximum(i * n_sub - 1, 0), 0)),
                  pl.BlockSpec((1, w * n_sub, dkv), lambda bi, i: (bi, i, 0))],
        out_specs=pl.BlockSpec((1, w * n_sub, dq), lambda bi, i: (bi, i, 0)),
        out_shape=jax.ShapeDtypeStruct((b, s, dq), BF16),
        compiler_params=_cparams(("parallel", "parallel")),
        name="sw_attn",
    )(sinks, q3, kv3, kv3)


def _first_max(vals):
    m = vals[0]
    for v in vals[1:]:
        m = jnp.maximum(m, v)
    idx = jnp.full(m.shape, len(vals) - 1, jnp.int32)
    for k in range(len(vals) - 2, -1, -1):
        idx = jnp.where(vals[k] == m, k, idx)
    return m, idx


def _route(lg):
    g = [lg[k:k + 1, :] for k in range(N_GROUPS)]
    gmax, gsel = _first_max(g)
    gsum = g[0] * 0.0
    for k in range(N_GROUPS):
        gsum = gsum + jnp.exp(g[k] - gmax)
    gprob = 1.0 / gsum
    within = []
    for k in range(EXPERTS_PER_GROUP):
        v = lg[N_GROUPS + (N_GROUPS - 1) * EXPERTS_PER_GROUP + k:N_GROUPS + (N_GROUPS - 1) * EXPERTS_PER_GROUP + k + 1, :]
        for grp in range(N_GROUPS - 2, -1, -1):
            r = N_GROUPS + grp * EXPERTS_PER_GROUP + k
            v = jnp.where(gsel == grp, lg[r:r + 1, :], v)
        within.append(v)
    m1, i1 = _first_max(within)
    rest = [jnp.where(i1 == k, -jnp.inf, within[k]) for k in range(EXPERTS_PER_GROUP)]
    m2, i2 = _first_max(rest)
    e2 = jnp.exp(m2 - m1)
    p1 = gprob / (1.0 + e2)
    p2 = gprob * e2 / (1.0 + e2)
    lo = jnp.minimum(i1, i2)
    hi = jnp.maximum(i1, i2)
    w_lo = jnp.where(i1 < i2, p1, p2)
    w_hi = jnp.where(i1 < i2, p2, p1)
    pair = jnp.where(lo == 0, hi - 1, jnp.where(lo == 1, hi + 1, 5))
    return gsel * N_PAIRS + pair, w_lo, w_hi


def _ffn_input(x1, mod_ref, gffn_ref):
    shift2 = mod_ref[0, 3:4, :]
    scale2 = mod_ref[0, 4:5, :]
    return (_rms(x1) * gffn_ref[...]) * (1.0 + scale2) + shift2


def _outproj_kernel(d_sb, osb_ref, osw_ref, x_ref, mod_ref, gsb_ref, gsw_ref, wo_ref, gffn_ref, wr_ref, tri_ref,
                    x1_ref, meta_ref, wts_ref, cnt_ref, run_ref):
    i = pl.program_id(0)
    tm = x_ref.shape[0]

    @pl.when(i == 0)
    def _():
        run_ref[...] = jnp.zeros_like(run_ref)

    gate1 = mod_ref[0, 2:3, :]
    a = (_rms(osb_ref[...].astype(F32)) * gsb_ref[...]).astype(BF16)
    bsw = (_rms(osw_ref[...].astype(F32)) * gsw_ref[...]).astype(BF16)
    mix = jnp.dot(a, wo_ref[:d_sb, :], preferred_element_type=F32) + jnp.dot(bsw, wo_ref[d_sb:, :],
                                                                                preferred_element_type=F32)
    x1 = x_ref[...] + gate1 * mix
    x1_ref[...] = x1
    h2 = _ffn_input(x1, mod_ref, gffn_ref)
    h2_hi = h2.astype(BF16)
    h2_lo = (h2 - h2_hi.astype(F32)).astype(BF16)
    wr2 = wr_ref[...]
    by_hi = _dot_nt(wr2, h2_hi)
    lg = by_hi[:META_ROWS] + by_hi[META_ROWS:] + _dot_nt(wr2[:META_ROWS], h2_lo)
    bucket, w_lo, w_hi = _route(lg)
    rows = lax.broadcasted_iota(jnp.int32, (META_ROWS, tm), 0)
    onehot = rows == bucket
    prefix = jnp.dot(jnp.where(onehot, 1.0, 0.0).astype(BF16), tri_ref[...], preferred_element_type=F32)
    run = run_ref[...]
    rank = jnp.sum(jnp.where(onehot, prefix - 1.0 + run, 0.0), axis=0, keepdims=True)
    run = run + jnp.sum(jnp.where(onehot, 1.0, 0.0), axis=1, keepdims=True)
    run_ref[...] = run
    cnt_ref[...] = jnp.broadcast_to(run, cnt_ref.shape)
    meta_ref[...] = jnp.concatenate([bucket, rank.astype(jnp.int32), jnp.zeros((6, tm), jnp.int32)], axis=0)
    wts_ref[...] = jnp.concatenate([w_lo, w_hi, jnp.zeros((6, tm), F32)], axis=0)


def _outproj(o_sb, o_sw, x2, mod3, g_sb, g_sw, wo_bf, g_ffn, wr, tri, tiles_per_batch):
    t, d = x2.shape
    d_sb = o_sb.shape[1]
    d_sw = o_sw.shape[1]
    tm = TM_ROUTE
    const = lambda shape: pl.BlockSpec(shape, lambda i: tuple(0 for _ in shape))
    return pl.pallas_call(
        functools.partial(_outproj_kernel, d_sb),
        grid=(t // tm,),
        in_specs=[pl.BlockSpec((tm, d_sb), lambda i: (i, 0)),
                  pl.BlockSpec((tm, d_sw), lambda i: (i, 0)),
                  pl.BlockSpec((tm, d), lambda i: (i, 0)),
                  pl.BlockSpec((1, 6, d), lambda i: (i // tiles_per_batch, 0, 0)),
                  const((1, d_sb)), const((1, d_sw)), const((d_sb + d_sw, d)), const((1, d)),
                  const((2 * META_ROWS, d)), const((tm, tm))],
        out_specs=[pl.BlockSpec((tm, d), lambda i: (i, 0)),
                   pl.BlockSpec((8, tm), lambda i: (0, i)),
                   pl.BlockSpec((8, tm), lambda i: (0, i)),
                   const((META_ROWS, LANES))],
        out_shape=[jax.ShapeDtypeStruct((t, d), F32),
                   jax.ShapeDtypeStruct((8, t), jnp.int32),
                   jax.ShapeDtypeStruct((8, t), F32),
                   jax.ShapeDtypeStruct((META_ROWS, LANES), F32)],
        scratch_shapes=[pltpu.VMEM((META_ROWS, 1), F32)],
        compiler_params=_cparams(("arbitrary",)),
        name="outproj",
    )(o_sb, o_sw, x2, mod3, g_sb, g_sw, wo_bf, g_ffn, wr, tri)


ROW_TILE = 8


def _tile_copy(src_ref, src_row, dst_ref, dst_row, sem):
    src = src_ref.at[pl.ds(pl.multiple_of(src_row * ROW_TILE, ROW_TILE), ROW_TILE), :]
    dst = dst_ref.at[pl.ds(pl.multiple_of(dst_row * ROW_TILE, ROW_TILE), ROW_TILE), :]
    return pltpu.make_async_copy(src, dst, sem)


def _wait_tiles(vmem_ref, hbm_ref, sem):
    n = vmem_ref.shape[0]
    pltpu.make_async_copy(hbm_ref.at[pl.ds(0, n), :], vmem_ref, sem).wait()


def _dispatch_kernel(dest_ref, zblk_ref, x1_ref, mod_ref, gffn_ref, wts_ref, out_ref,
                     rows, zero_blk, sems, zero_sem):
    i = pl.program_id(0)
    n = pl.num_programs(0)
    tm, d = x1_ref.shape
    words_per_row = d // 2
    slot = i % 2
    buf = rows.at[slot]

    @pl.when(i == 0)
    def _():
        zero_blk[...] = jnp.zeros_like(zero_blk)
        blk_rows = zero_blk.shape[0]

        def each_listed_block(act):
            def step(e, c):
                z = zblk_ref[e]

                @pl.when(z >= 0)
                def _():
                    dst = out_ref.at[pl.ds(pl.multiple_of(z * blk_rows, blk_rows), blk_rows), :]
                    act(pltpu.make_async_copy(zero_blk, dst, zero_sem))
                return c
            lax.fori_loop(0, zblk_ref.shape[0], step, 0)

        each_listed_block(lambda cp: cp.start())
        each_listed_block(lambda cp: cp.wait())

    @pl.when(i >= 2)
    def _():
        _wait_tiles(buf, out_ref, sems.at[slot])

    h2 = _ffn_input(x1_ref[...], mod_ref, gffn_ref)
    hi = pltpu.bitcast(h2[:, :words_per_row].astype(BF16).astype(F32), jnp.uint32)
    lo = pltpu.bitcast(h2[:, words_per_row:].astype(BF16).astype(F32), jnp.uint32)
    words = hi | (lo >> 16)
    n_word_rows = words_per_row // LANES
    for s in range(n_word_rows):
        buf[pl.ds(s, tm, stride=ROW_TILE), :] = words[:, s * LANES:(s + 1) * LANES]
    wts = jnp.concatenate([wts_ref[...], jnp.zeros((LANES - 8, tm), F32)], axis=0).T
    buf[pl.ds(n_word_rows, tm, stride=ROW_TILE), :] = pltpu.bitcast(wts, jnp.uint32)
    for s in range(n_word_rows + 1, ROW_TILE):
        buf[pl.ds(s, tm, stride=ROW_TILE), :] = jnp.zeros((tm, LANES), jnp.uint32)

    def issue(r, c):
        _tile_copy(buf, r, out_ref, dest_ref[i * tm + r], sems.at[slot]).start()
        return c

    lax.fori_loop(0, tm, issue, 0, unroll=8)

    @pl.when(i == n - 1)
    def _():
        _wait_tiles(buf, out_ref, sems.at[slot])

        @pl.when(n >= 2)
        def _():
            _wait_tiles(rows.at[1 - slot], out_ref, sems.at[1 - slot])


def _dispatch(dest, zero_blocks, x1, mod3, g_ffn, wts, p_rows, tiles_per_batch):
    t, d = x1.shape
    tm = TM_ROUTE
    return pl.pallas_call(
        _dispatch_kernel,
        grid_spec=pltpu.PrefetchScalarGridSpec(
            num_scalar_prefetch=2,
            grid=(t // tm,),
            in_specs=[pl.BlockSpec((tm, d), lambda i, *_: (i, 0)),
                      pl.BlockSpec((1, 6, d), lambda i, *_: (i // tiles_per_batch, 0, 0)),
                      pl.BlockSpec((1, d), lambda i, *_: (0, 0)),
                      pl.BlockSpec((8, tm), lambda i, *_: (0, i))],
            out_specs=pl.BlockSpec(memory_space=pl.ANY),
            scratch_shapes=[pltpu.VMEM((2, tm * ROW_TILE, LANES), jnp.uint32),
                            pltpu.VMEM((ROW_BLOCK * ROW_TILE, LANES), jnp.uint32),
                            pltpu.SemaphoreType.DMA((2,)), pltpu.SemaphoreType.DMA(())]),
        out_shape=jax.ShapeDtypeStruct((p_rows * ROW_TILE, LANES), jnp.uint32),
        compiler_params=_cparams(("arbitrary",)),
        name="dispatch",
    )(dest, zero_blocks, x1, mod3, g_ffn, wts)


def _experts_kernel(d, ea_ref, eb_ref, nvalid_ref, x_ref, wga_ref, wua_ref, wda_ref, wgb_ref, wub_ref, wdb_ref, y_ref):
    j = pl.program_id(0)
    rb = x_ref.shape[0] // ROW_TILE
    n_word_rows = d // 2 // LANES

    @pl.when(j < nvalid_ref[0])
    def _():
        high, low = [], []
        for s in range(n_word_rows):
            w = x_ref[pl.ds(s, rb, stride=ROW_TILE), :]
            high.append(pltpu.bitcast(w & jnp.uint32(0xFFFF0000), F32).astype(BF16))
            low.append(pltpu.bitcast(w << 16, F32).astype(BF16))
        xb = jnp.concatenate(high + low, axis=1)
        wts = pltpu.bitcast(x_ref[pl.ds(n_word_rows, rb, stride=ROW_TILE), :], F32)

        def hidden(wg_ref, wu_ref, wrow):
            gt = jnp.dot(xb, wg_ref[0], preferred_element_type=F32)
            up = jnp.dot(xb, wu_ref[0], preferred_element_type=F32)
            return ((gt * jax.nn.sigmoid(gt)) * up * wrow).astype(BF16)

        ha = hidden(wga_ref, wua_ref, wts[:, 0:1])
        hb = hidden(wgb_ref, wub_ref, wts[:, 1:2])
        y = (jnp.dot(ha, wda_ref[0], preferred_element_type=F32)
             + jnp.dot(hb, wdb_ref[0], preferred_element_type=F32))
        for c in range(ROW_TILE):
            y_ref[pl.ds(c, rb, stride=ROW_TILE), :] = y[:, c * LANES:(c + 1) * LANES]

    @pl.when(j >= nvalid_ref[0])
    def _():
        y_ref[...] = jnp.zeros_like(y_ref)


def _experts(blk_ea, blk_eb, nvalid, sorted_rows, wg_bf, wu_bf, wd_bf):
    n_e, d, d_e = wg_bf.shape
    assert d == ROW_TILE * LANES, "one expert output row must fill one (8, 128) f32 tile"
    rb = ROW_BLOCK
    nb = sorted_rows.shape[0] // (rb * ROW_TILE)
    up_a = pl.BlockSpec((1, d, d_e), lambda j, ea, eb, nv: (ea[j], 0, 0))
    up_b = pl.BlockSpec((1, d, d_e), lambda j, ea, eb, nv: (eb[j], 0, 0))
    dn_a = pl.BlockSpec((1, d_e, d), lambda j, ea, eb, nv: (ea[j], 0, 0))
    dn_b = pl.BlockSpec((1, d_e, d), lambda j, ea, eb, nv: (eb[j], 0, 0))
    rows_in = rows_out = pl.BlockSpec((rb * ROW_TILE, LANES), lambda j, ea, eb, nv: (j, 0))
    return pl.pallas_call(
        functools.partial(_experts_kernel, d),
        grid_spec=pltpu.PrefetchScalarGridSpec(
            num_scalar_prefetch=3,
            grid=(nb,),
            in_specs=[rows_in, up_a, up_a, dn_a, up_b, up_b, dn_b],
            out_specs=rows_out),
        out_shape=jax.ShapeDtypeStruct(sorted_rows.shape, F32),
        compiler_params=_cparams(("arbitrary",)),
        name="experts",
    )(blk_ea, blk_eb, nvalid, sorted_rows, wg_bf, wu_bf, wd_bf, wg_bf, wu_bf, wd_bf)


def _combine_kernel(dest_ref, x1_ref, mod_ref, g_ref, y_hbm, o_ref, ybuf, sems):
    i = pl.program_id(0)
    n = pl.num_programs(0)
    tm = x1_ref.shape[0]
    slot = i % 2

    def issue(tile, slot_):
        def one(r, c):
            _tile_copy(y_hbm, dest_ref[tile * tm + r], ybuf.at[slot_], r, sems.at[slot_]).start()
            return c
        lax.fori_loop(0, tm, one, 0, unroll=8)

    @pl.when(i == 0)
    def _():
        issue(0, 0)

    @pl.when(i + 1 < n)
    def _():
        issue(i + 1, 1 - slot)

    _wait_tiles(ybuf.at[slot], y_hbm, sems.at[slot])
    y = jnp.concatenate([ybuf[slot, pl.ds(c, tm, stride=ROW_TILE), :] for c in range(ROW_TILE)], axis=1)
    gate2 = mod_ref[0, 5:6, :]
    o_ref[...] = _rms(x1_ref[...] + gate2 * y) * g_ref[...]


def _combine(dest, x1, mod3, g_final, y_sorted, tiles_per_batch):
    t, d = x1.shape
    tm = TM_COMB
    return pl.pallas_call(
        _combine_kernel,
        grid_spec=pltpu.PrefetchScalarGridSpec(
            num_scalar_prefetch=1,
            grid=(t // tm,),
            in_specs=[pl.BlockSpec((tm, d), lambda i, dest: (i, 0)),
                      pl.BlockSpec((1, 6, d), lambda i, dest: (i // tiles_per_batch, 0, 0)),
                      pl.BlockSpec((1, d), lambda i, dest: (0, 0)),
                      pl.BlockSpec(memory_space=pl.ANY)],
            out_specs=pl.BlockSpec((tm, d), lambda i, dest: (i, 0)),
            scratch_shapes=[pltpu.VMEM((2, tm * ROW_TILE, LANES), F32), pltpu.SemaphoreType.DMA((2,))]),
        out_shape=jax.ShapeDtypeStruct((t, d), F32),
        compiler_params=_cparams(("arbitrary",)),
        name="combine",
    )(dest, x1, mod3, g_final, y_sorted)


def _pair_table():
    lo, hi = [], []
    for g in range(N_GROUPS):
        for a in range(EXPERTS_PER_GROUP):
            for b in range(a + 1, EXPERTS_PER_GROUP):
                lo.append(g * EXPERTS_PER_GROUP + a)
                hi.append(g * EXPERTS_PER_GROUP + b)
    return jnp.array(lo, jnp.int32), jnp.array(hi, jnp.int32)


def _layer(x, mod3, positions, norm_mix_g, w_in, sinks, out_norm_sb_g, out_norm_sw_g, w_out, norm_ffn_g,
           w_router_group, w_router_expert, w_gate, w_up, w_down, norm_final_g):
    b, s, d = x.shape
    t = b * s
    d_sb = N_SB_HEADS * HEAD_DIM
    d_sw = N_SW_HEADS * HEAD_DIM
    d_kv = N_SW_KV_HEADS * HEAD_DIM
    scale = HEAD_DIM ** -0.5
    x2 = x.reshape(t, d)

    n_tiles = d_sw // LANES
    sw_heads = jnp.array([h for j in range(n_tiles) for h in (j, j + n_tiles)], jnp.int32)
    sw_cols = (sw_heads[:, None] * HEAD_DIM + jnp.arange(HEAD_DIM, dtype=jnp.int32)[None, :]).reshape(-1)
    q_sw_cols = 3 * d_sb + sw_cols
    w_bf = jnp.concatenate([w_in[:, :d_sb] * (scale * LOG2_E), w_in[:, d_sb:3 * d_sb],
                            w_in[:, q_sw_cols] * scale, w_in[:, 3 * d_sb + d_sw:]], axis=1).astype(BF16)

    inv_freq = ROPE_THETA ** (-jnp.arange(0, HEAD_DIM, 2, dtype=F32) / HEAD_DIM)
    freq = jnp.tile(inv_freq, LANES // (HEAD_DIM // 2)).reshape(1, LANES)
    sign = jnp.tile(jnp.concatenate([-jnp.ones(HEAD_DIM // 2, F32), jnp.ones(HEAD_DIM // 2, F32)]),
                    LANES // HEAD_DIM).reshape(1, LANES)
    n_freq = HEAD_DIM // 2
    per_row = LANES // n_freq
    pos_dense = jnp.repeat(positions.reshape(t // per_row, per_row), n_freq, axis=1)
    cos_d, sin_d = _rope_tables(pos_dense, freq)
    cos_t = jnp.tile(cos_d.reshape(t, n_freq), (1, per_row))
    sin_t = jnp.tile(sin_d.reshape(t, n_freq), (1, per_row)) * sign

    qkv_sb, q_sw, kv_sw = _inproj(x2, mod3, norm_mix_g.reshape(1, d), w_bf, cos_t, sin_t, s // TM_PROJ,
                                  3 * d_sb, d_sw)

    tri_sb = (jnp.arange(SB_BLOCK)[:, None] > jnp.arange(SB_BLOCK)[None, :]).astype(BF16)
    o_sb = _sb_attention(qkv_sb.reshape(b, s, 3 * d_sb), tri_sb).reshape(t, d_sb)
    o_sw = _sw_attention(sinks, q_sw.reshape(b, s, d_sw), kv_sw.reshape(b, s, 2 * d_kv)).reshape(t, d_sw)

    wo_bf = jnp.concatenate([w_out[:d_sb], w_out[d_sb + sw_cols]], axis=0).astype(BF16)
    wr = jnp.concatenate([w_router_group.T, w_router_expert.T,
                          jnp.zeros((META_ROWS - N_GROUPS - N_GROUPS * EXPERTS_PER_GROUP, d), F32)], axis=0)
    wr_hi = wr.astype(BF16)
    wr = jnp.concatenate([wr_hi, (wr - wr_hi.astype(F32)).astype(BF16)], axis=0)
    tri_rt = (jnp.arange(TM_ROUTE)[:, None] <= jnp.arange(TM_ROUTE)[None, :]).astype(BF16)
    g_ffn = norm_ffn_g.reshape(1, d)
    x1, meta, wts, cnt = _outproj(o_sb, o_sw, x2, mod3, out_norm_sb_g.reshape(1, d_sb),
                                  out_norm_sw_g[sw_cols].reshape(1, d_sw), wo_bf, g_ffn, wr, tri_rt, s // TM_ROUTE)

    counts = cnt[:N_BUCKETS, 0].astype(jnp.int32)
    padded = (counts + ROW_BLOCK - 1) // ROW_BLOCK * ROW_BLOCK
    pend = jnp.cumsum(padded)
    pstart = pend - padded
    dest = pstart[meta[0]] + meta[1]
    p_rows = t + N_BUCKETS * ROW_BLOCK
    nb = p_rows // ROW_BLOCK
    blk_start = jnp.arange(nb, dtype=jnp.int32) * ROW_BLOCK
    blk_bucket = jnp.minimum(jnp.sum((pend[None, :] <= blk_start[:, None]).astype(jnp.int32), axis=1), N_BUCKETS - 1)
    e_lo, e_hi = _pair_table()
    nvalid = (pend[-1:] // ROW_BLOCK).astype(jnp.int32)

    last_blk = jnp.where(padded > 0, pend // ROW_BLOCK - 1, -1)
    spare_blk = nvalid + jnp.arange(N_BUCKETS, dtype=jnp.int32)
    zero_blocks = jnp.concatenate([last_blk, jnp.where(spare_blk < nb, spare_blk, -1)]).astype(jnp.int32)
    sorted_rows = _dispatch(dest, zero_blocks, x1, mod3, g_ffn, wts, p_rows, s // TM_ROUTE)
    y_sorted = _experts(e_lo[blk_bucket], e_hi[blk_bucket], nvalid, sorted_rows,
                        w_gate.astype(BF16), w_up.astype(BF16), w_down.astype(BF16))
    return _combine(dest, x1, mod3, norm_final_g.reshape(1, d), y_sorted, s // TM_COMB).reshape(b, s, d)


def kernel(x, c, positions, w_ada, b_ada, norm_mix_g, w_in, sinks, out_norm_sb_g, out_norm_sw_g, w_out, norm_ffn_g,
           w_router_group, w_router_expert, w_gate, w_up, w_down, norm_final_g):
    b, s, d = x.shape
    depth = w_ada.shape[0]
    assert depth == 1, "the final norm is fused into the layer's last stage"
    c_pad = jnp.pad(c, ((0, 8 - b), (0, 0)))
    mod = _adaln(c_pad, w_ada[0], b_ada[0].reshape(1, -1))
    mod3 = mod[:b].reshape(b, 6, d)
    return _layer(x, mod3, positions, norm_mix_g[0], w_in[0], sinks[0], out_norm_sb_g[0], out_norm_sw_g[0],
                  w_out[0], norm_ffn_g[0], w_router_group[0], w_router_expert[0], w_gate[0], w_up[0], w_down[0],
                  norm_final_g)
```

```python
import functools

import jax
import jax.numpy as jnp
from jax import lax
from jax.experimental import pallas as pl
from jax.experimental.pallas import tpu as pltpu

F32 = jnp.float32
BF16 = jnp.bfloat16

HEAD_DIM = 64
N_SB_HEADS = 8
N_SW_HEADS = 8
N_SW_KV_HEADS = 2
WINDOW = 128
ROPE_THETA = 10000.0
N_GROUPS = 4
EXPERTS_PER_GROUP = 4
PAIR_ROLES = ((0, 1), (0, 2), (0, 3), (1, 3), (1, 2), (3, 2))
N_PAIRS = len(PAIR_ROLES)
N_BUCKETS = N_GROUPS * N_PAIRS
EPS = 1e-6

LANES = 128
VMEM_LIMIT = 56 * 1024 * 1024
LOG2_E = 1.4426950408889634
SB_SKIP_LOG2 = -104.0 * LOG2_E

TM_PROJ = 512
SB_BLOCK = 256
ROW_BLOCK = 256
TM_ROUTE = 512
TM_COMB = 512
SB_DENSE_LEFT = 2
SB_BLOCKS_PER_STEP = 4
SW_BLOCKS_PER_STEP = 4
META_ROWS = 32


def _cparams(sem, vmem=VMEM_LIMIT):
    return pltpu.CompilerParams(dimension_semantics=sem, vmem_limit_bytes=vmem)


def _adaln_kernel(c_ref, w_ref, b_ref, o_ref):
    c = c_ref[...]
    s = c * jax.nn.sigmoid(c)
    o_ref[...] = jnp.dot(s, w_ref[...], precision=lax.Precision.HIGHEST,
                         preferred_element_type=F32) + b_ref[...]


def _adaln(c_pad, w, b):
    rows, d = c_pad.shape
    n = w.shape[1]
    tn = 1024
    return pl.pallas_call(
        _adaln_kernel,
        grid=(n // tn,),
        in_specs=[pl.BlockSpec((rows, d), lambda j: (0, 0)),
                  pl.BlockSpec((d, tn), lambda j: (0, j)),
                  pl.BlockSpec((1, tn), lambda j: (0, j))],
        out_specs=pl.BlockSpec((rows, tn), lambda j: (0, j)),
        out_shape=jax.ShapeDtypeStruct((rows, n), F32),
        compiler_params=_cparams(("parallel",)),
        name="adaln",
    )(c_pad, w, b)


def _rope_kernel(pos_ref, freq_ref, cos_ref, sin_ref):
    ang = pos_ref[...].astype(F32) * freq_ref[...]
    cos_ref[...] = jnp.cos(ang)
    sin_ref[...] = jnp.sin(ang)


def _rope_tables(pos_dense, freq):
    rows = pos_dense.shape[0]
    tm = min(1024, rows)
    row = pl.BlockSpec((tm, LANES), lambda i: (i, 0))
    vec = pl.BlockSpec((1, LANES), lambda i: (0, 0))
    return pl.pallas_call(
        _rope_kernel,
        grid=(rows // tm,),
        in_specs=[row, vec],
        out_specs=[row, row],
        out_shape=[jax.ShapeDtypeStruct((rows, LANES), F32)] * 2,
        compiler_params=_cparams(("parallel",)),
        name="rope",
    )(pos_dense, freq)


def _rms(x):
    return x * lax.rsqrt(jnp.mean(x * x, axis=-1, keepdims=True) + EPS)


def _rotate_half_pairs(x):
    lane = lax.broadcasted_iota(jnp.int32, x.shape, 1)
    first_half = (lane % HEAD_DIM) < (HEAD_DIM // 2)
    return jnp.where(first_half, pltpu.roll(x, LANES - HEAD_DIM // 2, 1), pltpu.roll(x, HEAD_DIM // 2, 1))


def _inproj_kernel(d_sb3, d_swq, x_ref, mod_ref, g_ref, w_ref, cos_ref, sin_ref, sb_ref, q_ref, kv_ref):
    x = x_ref[...]
    shift = mod_ref[0, 0:1, :]
    scale = mod_ref[0, 1:2, :]
    h = (_rms(x) * g_ref[...]) * (1.0 + scale) + shift
    hb = h.astype(BF16)
    chunk = 512
    for c0 in range(0, d_sb3, chunk):
        sb_ref[:, c0:c0 + chunk] = jnp.dot(hb, w_ref[:, c0:c0 + chunk],
                                           preferred_element_type=F32).astype(BF16)
    cos = cos_ref[...]
    sin = sin_ref[...]
    q = jnp.dot(hb, w_ref[:, d_sb3:d_sb3 + d_swq], preferred_element_type=F32)
    for c0 in range(0, d_swq, LANES):
        qt = q[:, c0:c0 + LANES]
        q_ref[:, c0:c0 + LANES] = (qt * cos + _rotate_half_pairs(qt) * sin).astype(BF16)
    kv = jnp.dot(hb, w_ref[:, d_sb3 + d_swq:], preferred_element_type=F32)
    k = kv[:, :LANES]
    kv_ref[:, :LANES] = (k * cos + _rotate_half_pairs(k) * sin).astype(BF16)
    kv_ref[:, LANES:] = kv[:, LANES:].astype(BF16)


def _inproj(x2, mod3, g, w_bf, cos_t, sin_t, tiles_per_batch, d_sb3, d_swq):
    t, d = x2.shape
    d_in = w_bf.shape[1]
    d_kv = d_in - d_sb3 - d_swq
    tm = TM_PROJ
    return pl.pallas_call(
        functools.partial(_inproj_kernel, d_sb3, d_swq),
        grid=(t // tm,),
        in_specs=[pl.BlockSpec((tm, d), lambda i: (i, 0)),
                  pl.BlockSpec((1, 6, d), lambda i: (i // tiles_per_batch, 0, 0)),
                  pl.BlockSpec((1, d), lambda i: (0, 0)),
                  pl.BlockSpec((d, d_in), lambda i: (0, 0)),
                  pl.BlockSpec((tm, LANES), lambda i: (i, 0)),
                  pl.BlockSpec((tm, LANES), lambda i: (i, 0))],
        out_specs=[pl.BlockSpec((tm, d_sb3), lambda i: (i, 0)),
                   pl.BlockSpec((tm, d_swq), lambda i: (i, 0)),
                   pl.BlockSpec((tm, d_kv), lambda i: (i, 0))],
        out_shape=[jax.ShapeDtypeStruct((t, d_sb3), BF16),
                   jax.ShapeDtypeStruct((t, d_swq), BF16),
                   jax.ShapeDtypeStruct((t, d_kv), BF16)],
        compiler_params=_cparams(("parallel",)),
        name="inproj",
    )(x2, mod3, g, w_bf, cos_t, sin_t)


def _dot_nt(a, b):
    return lax.dot_general(a, b, (((1,), (1,)), ((), ())), preferred_element_type=F32)


def _sb_block(qm, k, v, tri, carry, causal):
    z = _dot_nt(qm, k)
    log_beta = jnp.minimum(z, 0.0) - jnp.log2(1.0 + jnp.exp2(-jnp.abs(z)))
    log_rest = log_beta - z
    if causal is not None:
        log_rest = jnp.where(causal, log_rest, 0.0)
    later = jnp.dot(log_rest.astype(BF16), tri, preferred_element_type=F32)
    w = jnp.exp2(log_beta + later + carry)
    if causal is not None:
        w = jnp.where(causal, w, 0.0)
    out = jnp.dot(w.astype(BF16), v, preferred_element_type=F32)
    return out, carry + jnp.sum(log_rest, axis=1, keepdims=True)


def _sb_kernel(blk, n_q, q_ref, k_ref, v_ref, tri_ref, o_ref):
    i = pl.program_id(2)
    tri = tri_ref[...]
    lane = lax.broadcasted_iota(jnp.int32, (blk, LANES), 1)
    row = lax.broadcasted_iota(jnp.int32, (2 * blk, blk), 0)
    col = lax.broadcasted_iota(jnp.int32, (2 * blk, blk), 1)
    causal = col < (row % blk)

    def sweep(qs, kb, carry, causal_mask):
        s = pl.multiple_of(kb * blk, blk)
        return _sb_block(qs, k_ref[0, pl.ds(s, blk), :], v_ref[0, pl.ds(s, blk), :], tri, carry, causal_mask)

    dense = []
    for j in range(n_q):
        g = i * n_q + j
        q = q_ref[0, j * blk:(j + 1) * blk, :]
        qs = jnp.concatenate([jnp.where((lane >= h * HEAD_DIM) & (lane < (h + 1) * HEAD_DIM), q, jnp.zeros_like(q))
                              for h in range(2)], axis=0)
        acc, carry = sweep(qs, g, jnp.zeros((2 * blk, 1), F32), causal)
        for left in range(1, SB_DENSE_LEFT + 1):
            out, carry = sweep(qs, jnp.maximum(g - left, 0), jnp.where(g >= left, carry, -1e30), None)
            acc = acc + out
        dense.append((g, qs, carry, acc))

    for j, (g, qs, carry, acc) in enumerate(dense):
        def cond(state):
            kb, carry, _ = state
            return jnp.logical_and(kb >= 0, jnp.max(carry) >= SB_SKIP_LOG2)

        def body(state, qs=qs):
            kb, carry, acc = state
            out, carry = sweep(qs, kb, carry, None)
            return kb - 1, carry, acc + out

        _, _, acc = lax.while_loop(cond, body, (g - 1 - SB_DENSE_LEFT, carry, acc))
        o_ref[0, j * blk:(j + 1) * blk, :] = jnp.where(lane < HEAD_DIM, acc[:blk], acc[blk:]).astype(o_ref.dtype)


def _sb_attention(qkv3, tri):
    b, s, d3 = qkv3.shape
    d_sb = d3 // 3
    n_pairs = d_sb // LANES
    blk = SB_BLOCK
    n_q = SB_BLOCKS_PER_STEP
    return pl.pallas_call(
        functools.partial(_sb_kernel, blk, n_q),
        grid=(b, n_pairs, s // (blk * n_q)),
        in_specs=[pl.BlockSpec((1, blk * n_q, LANES), lambda bi, p, i: (bi, i, p)),
                  pl.BlockSpec((1, s, LANES), lambda bi, p, i: (bi, 0, n_pairs + p)),
                  pl.BlockSpec((1, s, LANES), lambda bi, p, i: (bi, 0, 2 * n_pairs + p)),
                  pl.BlockSpec((blk, blk), lambda bi, p, i: (0, 0))],
        out_specs=pl.BlockSpec((1, blk * n_q, LANES), lambda bi, p, i: (bi, i, p)),
        out_shape=jax.ShapeDtypeStruct((b, s, d_sb), BF16),
        compiler_params=_cparams(("parallel", "parallel", "arbitrary")),
        name="sb_attn",
    )(qkv3, qkv3, qkv3, tri)


def _sw_kernel(n_tiles, n_sub, sinks_ref, q_ref, kvp_ref, kvc_ref, o_ref):
    i = pl.program_id(1)
    w = WINDOW
    qi = lax.broadcasted_iota(jnp.int32, (w, 2 * w), 0)
    kj = lax.broadcasted_iota(jnp.int32, (w, 2 * w), 1)
    diff = w + qi - kj
    in_window = (diff >= 0) & (diff < w)
    lane = lax.broadcasted_iota(jnp.int32, (w, LANES), 1)
    for j in range(n_sub):
        if j == 0:
            kk = jnp.concatenate([kvp_ref[0, :, :LANES], kvc_ref[0, :w, :LANES]], axis=0)
            vv = jnp.concatenate([kvp_ref[0, :, LANES:], kvc_ref[0, :w, LANES:]], axis=0)
            mask = in_window & ((i * n_sub) * w + kj - w >= 0)
        else:
            kk = kvc_ref[0, (j - 1) * w:(j + 1) * w, :LANES]
            vv = kvc_ref[0, (j - 1) * w:(j + 1) * w, LANES:]
            mask = in_window
        for t in range(n_tiles):
            qt = q_ref[0, j * w:(j + 1) * w, t * LANES:(t + 1) * LANES]
            outs = []
            for half in range(2):
                head = t + half * n_tiles
                in_head = (lane >= half * HEAD_DIM) & (lane < (half + 1) * HEAD_DIM)
                qm = jnp.where(in_head, qt, jnp.zeros_like(qt))
                s = jnp.where(mask, _dot_nt(qm, kk), -1e30)
                sink = sinks_ref[head]
                m = jnp.maximum(jnp.max(s, axis=-1, keepdims=True), sink)
                p = jnp.exp(s - m)
                denom = jnp.sum(p, axis=-1, keepdims=True) + jnp.exp(sink - m)
                outs.append(jnp.dot(p.astype(BF16), vv, preferred_element_type=F32) * (1.0 / denom))
            o_ref[0, j * w:(j + 1) * w, t * LANES:(t + 1) * LANES] = jnp.where(
                lane < HEAD_DIM, outs[0], outs[1]).astype(o_ref.dtype)


def _sw_attention(sinks, q3, kv3):
    b, s, dq = q3.shape
    dkv = kv3.shape[2]
    w = WINDOW
    n_sub = SW_BLOCKS_PER_STEP
    n_tiles = dq // LANES
    return pl.pallas_call(
        functools.partial(_sw_kernel, n_tiles, n_sub),
        grid=(b, s // (w * n_sub)),
        in_specs=[pl.BlockSpec(memory_space=pltpu.SMEM),
                  pl.BlockSpec((1, w * n_sub, dq), lambda bi, i: (bi, i, 0)),
                  pl.BlockSpec((1, w, dkv), lambda bi, i: (bi, jnp.maximum(i * n_sub - 1, 0), 0)),
                  pl.BlockSpec((1, w * n_sub, dkv), lambda bi, i: (bi, i, 0))],
        out_specs=pl.BlockSpec((1, w * n_sub, dq), lambda bi, i: (bi, i, 0)),
        out_shape=jax.ShapeDtypeStruct((b, s, dq), BF16),
        compiler_params=_cparams(("parallel", "parallel")),
        name="sw_attn",
    )(sinks, q3, kv3, kv3)


def _first_max(vals):
    m = vals[0]
    for v in vals[1:]:
        m = jnp.maximum(m, v)
    idx = jnp.full(m.shape, len(vals) - 1, jnp.int32)
    for k in range(len(vals) - 2, -1, -1):
        idx = jnp.where(vals[k] == m, k, idx)
    return m, idx


def _route(lg):
    g = [lg[k:k + 1, :] for k in range(N_GROUPS)]
    gmax, gsel = _first_max(g)
    gsum = g[0] * 0.0
    for k in range(N_GROUPS):
        gsum = gsum + jnp.exp(g[k] - gmax)
    gprob = 1.0 / gsum
    within = []
    for k in range(EXPERTS_PER_GROUP):
        v = lg[N_GROUPS + (N_GROUPS - 1) * EXPERTS_PER_GROUP + k:N_GROUPS + (N_GROUPS - 1) * EXPERTS_PER_GROUP + k + 1, :]
        for grp in range(N_GROUPS - 2, -1, -1):
            r = N_GROUPS + grp * EXPERTS_PER_GROUP + k
            v = jnp.where(gsel == grp, lg[r:r + 1, :], v)
        within.append(v)
    m1, i1 = _first_max(within)
    rest = [jnp.where(i1 == k, -jnp.inf, within[k]) for k in range(EXPERTS_PER_GROUP)]
    m2, i2 = _first_max(rest)
    e2 = jnp.exp(m2 - m1)
    p1 = gprob / (1.0 + e2)
    p2 = gprob * e2 / (1.0 + e2)
    pair = jnp.zeros_like(i1)
    w_a = jnp.zeros_like(p1)
    w_b = jnp.zeros_like(p1)
    for p, (a, b) in enumerate(PAIR_ROLES):
        fwd = (i1 == a) & (i2 == b)
        rev = (i1 == b) & (i2 == a)
        pair = jnp.where(fwd | rev, p, pair)
        w_a = jnp.where(fwd, p1, jnp.where(rev, p2, w_a))
        w_b = jnp.where(fwd, p2, jnp.where(rev, p1, w_b))
    return gsel * N_PAIRS + pair, w_a, w_b


ROW_TILE = 8


def _write_row_tiles(buf, h2, w_rows):
    tm, d = h2.shape
    words_per_row = d // 2
    hi = pltpu.bitcast(h2[:, :words_per_row].astype(BF16).astype(F32), jnp.uint32)
    lo = pltpu.bitcast(h2[:, words_per_row:].astype(BF16).astype(F32), jnp.uint32)
    words = hi | (lo >> 16)
    n_word_rows = words_per_row // LANES
    for s in range(n_word_rows):
        buf[pl.ds(s, tm, stride=ROW_TILE), :] = words[:, s * LANES:(s + 1) * LANES]
    buf[pl.ds(n_word_rows, tm, stride=ROW_TILE), :] = pltpu.bitcast(w_rows.T, jnp.uint32)
    for s in range(n_word_rows + 1, ROW_TILE):
        buf[pl.ds(s, tm, stride=ROW_TILE), :] = jnp.zeros((tm, LANES), jnp.uint32)


def _ffn_input(x1, mod_ref, gffn_ref):
    shift2 = mod_ref[0, 3:4, :]
    scale2 = mod_ref[0, 4:5, :]
    return (_rms(x1) * gffn_ref[...]) * (1.0 + scale2) + shift2


def _outproj_kernel(d_sb, osb_ref, osw_ref, x_ref, mod_ref, gsb_ref, gsw_ref, wo_ref, gffn_ref, wr_ref, tri_ref,
                    x1_ref, meta_ref, tiles_ref, cnt_ref, run_ref):
    i = pl.program_id(0)
    tm = x_ref.shape[0]

    @pl.when(i == 0)
    def _():
        run_ref[...] = jnp.zeros_like(run_ref)

    gate1 = mod_ref[0, 2:3, :]
    a = (_rms(osb_ref[...].astype(F32)) * gsb_ref[...]).astype(BF16)
    bsw = (_rms(osw_ref[...].astype(F32)) * gsw_ref[...]).astype(BF16)
    mix = jnp.dot(a, wo_ref[:d_sb, :], preferred_element_type=F32) + jnp.dot(bsw, wo_ref[d_sb:, :],
                                                                                preferred_element_type=F32)
    x1 = x_ref[...] + gate1 * mix
    x1_ref[...] = x1
    h2 = _ffn_input(x1, mod_ref, gffn_ref)
    h2_hi = h2.astype(BF16)
    h2_lo = (h2 - h2_hi.astype(F32)).astype(BF16)
    wr2 = wr_ref[...]
    by_hi = _dot_nt(wr2, h2_hi)
    lg = by_hi[:META_ROWS] + by_hi[META_ROWS:] + _dot_nt(wr2[:META_ROWS], h2_lo)
    bucket, w_a, w_b = _route(lg)
    rows = lax.broadcasted_iota(jnp.int32, (META_ROWS, tm), 0)
    onehot = rows == bucket
    prefix = jnp.dot(jnp.where(onehot, 1.0, 0.0).astype(BF16), tri_ref[...], preferred_element_type=F32)
    run = run_ref[...]
    rank = jnp.sum(jnp.where(onehot, prefix - 1.0 + run, 0.0), axis=0, keepdims=True)
    run = run + jnp.sum(jnp.where(onehot, 1.0, 0.0), axis=1, keepdims=True)
    run_ref[...] = run
    cnt_ref[...] = jnp.broadcast_to(run, cnt_ref.shape)
    meta_ref[...] = jnp.concatenate([bucket, rank.astype(jnp.int32), jnp.zeros((6, tm), jnp.int32)], axis=0)
    _write_row_tiles(tiles_ref, h2, jnp.concatenate([w_a, w_b, jnp.zeros((LANES - 2, tm), F32)], axis=0))


def _outproj(o_sb, o_sw, x2, mod3, g_sb, g_sw, wo_bf, g_ffn, wr, tri, tiles_per_batch):
    t, d = x2.shape
    d_sb = o_sb.shape[1]
    d_sw = o_sw.shape[1]
    tm = TM_ROUTE
    const = lambda shape: pl.BlockSpec(shape, lambda i: tuple(0 for _ in shape))
    return pl.pallas_call(
        functools.partial(_outproj_kernel, d_sb),
        grid=(t // tm,),
        in_specs=[pl.BlockSpec((tm, d_sb), lambda i: (i, 0)),
                  pl.BlockSpec((tm, d_sw), lambda i: (i, 0)),
                  pl.BlockSpec((tm, d), lambda i: (i, 0)),
                  pl.BlockSpec((1, 6, d), lambda i: (i // tiles_per_batch, 0, 0)),
                  const((1, d_sb)), const((1, d_sw)), const((d_sb + d_sw, d)), const((1, d)),
                  const((2 * META_ROWS, d)), const((tm, tm))],
        out_specs=[pl.BlockSpec((tm, d), lambda i: (i, 0)),
                   pl.BlockSpec((8, tm), lambda i: (0, i)),
                   pl.BlockSpec((tm * ROW_TILE, LANES), lambda i: (i, 0)),
                   const((META_ROWS, LANES))],
        out_shape=[jax.ShapeDtypeStruct((t, d), F32),
                   jax.ShapeDtypeStruct((8, t), jnp.int32),
                   jax.ShapeDtypeStruct((t * ROW_TILE, LANES), jnp.uint32),
                   jax.ShapeDtypeStruct((META_ROWS, LANES), F32)],
        scratch_shapes=[pltpu.VMEM((META_ROWS, 1), F32)],
        compiler_params=_cparams(("arbitrary",)),
        name="outproj",
    )(o_sb, o_sw, x2, mod3, g_sb, g_sw, wo_bf, g_ffn, wr, tri)


def _tile_copy(src_ref, src_row, dst_ref, dst_row, sem):
    src = src_ref.at[pl.ds(pl.multiple_of(src_row * ROW_TILE, ROW_TILE), ROW_TILE), :]
    dst = dst_ref.at[pl.ds(pl.multiple_of(dst_row * ROW_TILE, ROW_TILE), ROW_TILE), :]
    return pltpu.make_async_copy(src, dst, sem)


def _for_rows_with_index(n_rows, read_index, act, group=8):
    def body(g, c):
        rows = [g * group + u for u in range(group)]
        idx = [read_index(r) for r in rows]
        for r, ix in zip(rows, idx):
            act(r, ix)
        return c
    lax.fori_loop(0, n_rows // group, body, 0)


def _wait_tiles(like_ref, hbm_ref, sem):
    n = like_ref.shape[0]
    pltpu.make_async_copy(hbm_ref.at[pl.ds(0, n), :], like_ref, sem).wait()


def _dispatch_kernel(tm, dest_ref, zblk_ref, tiles_hbm, out_ref, zero_blk, sems, zero_sem):
    i = pl.program_id(0)
    n = pl.num_programs(0)
    slot = i % 2

    @pl.when(i == 0)
    def _():
        zero_blk[...] = jnp.zeros_like(zero_blk)
        blk_rows = zero_blk.shape[0]

        def each_listed_block(act):
            def step(e, c):
                z = zblk_ref[e]

                @pl.when(z >= 0)
                def _():
                    dst = out_ref.at[pl.ds(pl.multiple_of(z * blk_rows, blk_rows), blk_rows), :]
                    act(pltpu.make_async_copy(zero_blk, dst, zero_sem))
                return c
            lax.fori_loop(0, zblk_ref.shape[0], step, 0)

        each_listed_block(lambda cp: cp.start())
        each_listed_block(lambda cp: cp.wait())

    def wait_step(slot_):
        batch = tiles_hbm.at[pl.ds(0, tm * ROW_TILE), :]
        pltpu.make_async_copy(batch, out_ref.at[pl.ds(0, tm * ROW_TILE), :], sems.at[slot_]).wait()

    @pl.when(i >= 2)
    def _():
        wait_step(slot)

    def issue(r, dst_row):
        _tile_copy(tiles_hbm, i * tm + r, out_ref, dst_row, sems.at[slot]).start()

    _for_rows_with_index(tm, lambda r: dest_ref[i * tm + r], issue)

    @pl.when(i == n - 1)
    def _():
        wait_step(slot)

        @pl.when(n >= 2)
        def _():
            wait_step(1 - slot)


def _dispatch(dest, zero_blocks, tiles, p_rows):
    t = tiles.shape[0] // ROW_TILE
    tm = TM_ROUTE
    return pl.pallas_call(
        functools.partial(_dispatch_kernel, tm),
        grid_spec=pltpu.PrefetchScalarGridSpec(
            num_scalar_prefetch=2,
            grid=(t // tm,),
            in_specs=[pl.BlockSpec(memory_space=pl.ANY)],
            out_specs=pl.BlockSpec(memory_space=pl.ANY),
            scratch_shapes=[pltpu.VMEM((ROW_BLOCK * ROW_TILE, LANES), jnp.uint32),
                            pltpu.SemaphoreType.DMA((2,)), pltpu.SemaphoreType.DMA(())]),
        out_shape=jax.ShapeDtypeStruct((p_rows * ROW_TILE, LANES), jnp.uint32),
        compiler_params=_cparams(("arbitrary",)),
        name="dispatch",
    )(dest, zero_blocks, tiles)


def _experts_kernel(d, ea_ref, eb_ref, nvalid_ref, x_ref, wga_ref, wua_ref, wda_ref, wgb_ref, wub_ref, wdb_ref, y_ref):
    j = pl.program_id(0)
    rb = x_ref.shape[0] // ROW_TILE
    n_word_rows = d // 2 // LANES

    @pl.when(j < nvalid_ref[0])
    def _():
        high, low = [], []
        for s in range(n_word_rows):
            w = x_ref[pl.ds(s, rb, stride=ROW_TILE), :]
            high.append(pltpu.bitcast(w & jnp.uint32(0xFFFF0000), F32).astype(BF16))
            low.append(pltpu.bitcast(w << 16, F32).astype(BF16))
        xb = jnp.concatenate(high + low, axis=1)
        wts = pltpu.bitcast(x_ref[pl.ds(n_word_rows, rb, stride=ROW_TILE), :], F32)

        def hidden(wg_ref, wu_ref, wrow):
            gt = jnp.dot(xb, wg_ref[0], preferred_element_type=F32)
            up = jnp.dot(xb, wu_ref[0], preferred_element_type=F32)
            return ((gt * jax.nn.sigmoid(gt)) * up * wrow).astype(BF16)

        ha = hidden(wga_ref, wua_ref, wts[:, 0:1])
        hb = hidden(wgb_ref, wub_ref, wts[:, 1:2])
        y = (jnp.dot(ha, wda_ref[0], preferred_element_type=F32)
             + jnp.dot(hb, wdb_ref[0], preferred_element_type=F32))
        for c in range(ROW_TILE):
            y_ref[pl.ds(c, rb, stride=ROW_TILE), :] = y[:, c * LANES:(c + 1) * LANES]

    @pl.when(j >= nvalid_ref[0])
    def _():
        y_ref[...] = jnp.zeros_like(y_ref)


def _experts(blk_ea, blk_eb, nvalid, sorted_rows, wg_bf, wu_bf, wd_bf):
    n_e, d, d_e = wg_bf.shape
    assert d == ROW_TILE * LANES, "one expert output row must fill one (8, 128) f32 tile"
    rb = ROW_BLOCK
    nb = sorted_rows.shape[0] // (rb * ROW_TILE)
    up_a = pl.BlockSpec((1, d, d_e), lambda j, ea, eb, nv: (ea[j], 0, 0))
    up_b = pl.BlockSpec((1, d, d_e), lambda j, ea, eb, nv: (eb[j], 0, 0))
    dn_a = pl.BlockSpec((1, d_e, d), lambda j, ea, eb, nv: (ea[j], 0, 0))
    dn_b = pl.BlockSpec((1, d_e, d), lambda j, ea, eb, nv: (eb[j], 0, 0))
    rows_in = rows_out = pl.BlockSpec((rb * ROW_TILE, LANES), lambda j, ea, eb, nv: (j, 0))
    return pl.pallas_call(
        functools.partial(_experts_kernel, d),
        grid_spec=pltpu.PrefetchScalarGridSpec(
            num_scalar_prefetch=3,
            grid=(nb,),
            in_specs=[rows_in, up_a, up_a, dn_a, up_b, up_b, dn_b],
            out_specs=rows_out),
        out_shape=jax.ShapeDtypeStruct(sorted_rows.shape, F32),
        compiler_params=_cparams(("arbitrary",)),
        name="experts",
    )(blk_ea, blk_eb, nvalid, sorted_rows, wg_bf, wu_bf, wd_bf, wg_bf, wu_bf, wd_bf)


def _combine_kernel(dest_ref, x1_ref, mod_ref, g_ref, y_hbm, o_ref, ybuf, sems):
    i = pl.program_id(0)
    n = pl.num_programs(0)
    tm = x1_ref.shape[0]
    slot = i % 2

    def issue(tile, slot_):
        def one(r, src_row):
            _tile_copy(y_hbm, src_row, ybuf.at[slot_], r, sems.at[slot_]).start()
        _for_rows_with_index(tm, lambda r: dest_ref[tile * tm + r], one)

    @pl.when(i == 0)
    def _():
        issue(0, 0)

    @pl.when(i + 1 < n)
    def _():
        issue(i + 1, 1 - slot)

    _wait_tiles(ybuf.at[slot], y_hbm, sems.at[slot])
    y = jnp.concatenate([ybuf[slot, pl.ds(c, tm, stride=ROW_TILE), :] for c in range(ROW_TILE)], axis=1)
    gate2 = mod_ref[0, 5:6, :]
    o_ref[...] = _rms(x1_ref[...] + gate2 * y) * g_ref[...]


def _combine(dest, x1, mod3, g_final, y_sorted, tiles_per_batch):
    t, d = x1.shape
    tm = TM_COMB
    return pl.pallas_call(
        _combine_kernel,
        grid_spec=pltpu.PrefetchScalarGridSpec(
            num_scalar_prefetch=1,
            grid=(t // tm,),
            in_specs=[pl.BlockSpec((tm, d), lambda i, dest: (i, 0)),
                      pl.BlockSpec((1, 6, d), lambda i, dest: (i // tiles_per_batch, 0, 0)),
                      pl.BlockSpec((1, d), lambda i, dest: (0, 0)),
                      pl.BlockSpec(memory_space=pl.ANY)],
            out_specs=pl.BlockSpec((tm, d), lambda i, dest: (i, 0)),
            scratch_shapes=[pltpu.VMEM((2, tm * ROW_TILE, LANES), F32), pltpu.SemaphoreType.DMA((2,))]),
        out_shape=jax.ShapeDtypeStruct((t, d), F32),
        compiler_params=_cparams(("arbitrary",)),
        name="combine",
    )(dest, x1, mod3, g_final, y_sorted)


def _pair_table():
    ea = [g * EXPERTS_PER_GROUP + a for g in range(N_GROUPS) for a, _ in PAIR_ROLES]
    eb = [g * EXPERTS_PER_GROUP + b for g in range(N_GROUPS) for _, b in PAIR_ROLES]
    return jnp.array(ea, jnp.int32), jnp.array(eb, jnp.int32)


def _layer(x, mod3, positions, norm_mix_g, w_in, sinks, out_norm_sb_g, out_norm_sw_g, w_out, norm_ffn_g,
           w_router_group, w_router_expert, w_gate, w_up, w_down, norm_final_g):
    b, s, d = x.shape
    t = b * s
    d_sb = N_SB_HEADS * HEAD_DIM
    d_sw = N_SW_HEADS * HEAD_DIM
    d_kv = N_SW_KV_HEADS * HEAD_DIM
    scale = HEAD_DIM ** -0.5
    x2 = x.reshape(t, d)

    n_tiles = d_sw // LANES
    sw_heads = jnp.array([h for j in range(n_tiles) for h in (j, j + n_tiles)], jnp.int32)
    sw_cols = (sw_heads[:, None] * HEAD_DIM + jnp.arange(HEAD_DIM, dtype=jnp.int32)[None, :]).reshape(-1)
    q_sw_cols = 3 * d_sb + sw_cols
    w_bf = jnp.concatenate([w_in[:, :d_sb] * (scale * LOG2_E), w_in[:, d_sb:3 * d_sb],
                            w_in[:, q_sw_cols] * scale, w_in[:, 3 * d_sb + d_sw:]], axis=1).astype(BF16)

    inv_freq = ROPE_THETA ** (-jnp.arange(0, HEAD_DIM, 2, dtype=F32) / HEAD_DIM)
    freq = jnp.tile(inv_freq, LANES // (HEAD_DIM // 2)).reshape(1, LANES)
    sign = jnp.tile(jnp.concatenate([-jnp.ones(HEAD_DIM // 2, F32), jnp.ones(HEAD_DIM // 2, F32)]),
                    LANES // HEAD_DIM).reshape(1, LANES)
    n_freq = HEAD_DIM // 2
    per_row = LANES // n_freq
    pos_dense = jnp.repeat(positions.reshape(t // per_row, per_row), n_freq, axis=1)
    cos_d, sin_d = _rope_tables(pos_dense, freq)
    cos_t = jnp.tile(cos_d.reshape(t, n_freq), (1, per_row))
    sin_t = jnp.tile(sin_d.reshape(t, n_freq), (1, per_row)) * sign

    qkv_sb, q_sw, kv_sw = _inproj(x2, mod3, norm_mix_g.reshape(1, d), w_bf, cos_t, sin_t, s // TM_PROJ,
                                  3 * d_sb, d_sw)

    tri_sb = (jnp.arange(SB_BLOCK)[:, None] > jnp.arange(SB_BLOCK)[None, :]).astype(BF16)
    o_sb = _sb_attention(qkv_sb.reshape(b, s, 3 * d_sb), tri_sb).reshape(t, d_sb)
    o_sw = _sw_attention(sinks, q_sw.reshape(b, s, d_sw), kv_sw.reshape(b, s, 2 * d_kv)).reshape(t, d_sw)

    wo_bf = jnp.concatenate([w_out[:d_sb], w_out[d_sb + sw_cols]], axis=0).astype(BF16)
    wr = jnp.concatenate([w_router_group.T, w_router_expert.T,
                          jnp.zeros((META_ROWS - N_GROUPS - N_GROUPS * EXPERTS_PER_GROUP, d), F32)], axis=0)
    wr_hi = wr.astype(BF16)
    wr = jnp.concatenate([wr_hi, (wr - wr_hi.astype(F32)).astype(BF16)], axis=0)
    tri_rt = (jnp.arange(TM_ROUTE)[:, None] <= jnp.arange(TM_ROUTE)[None, :]).astype(BF16)
    g_ffn = norm_ffn_g.reshape(1, d)
    x1, meta, tiles, cnt = _outproj(o_sb, o_sw, x2, mod3, out_norm_sb_g.reshape(1, d_sb),
                                  out_norm_sw_g[sw_cols].reshape(1, d_sw), wo_bf, g_ffn, wr, tri_rt, s // TM_ROUTE)

    counts = cnt[:N_BUCKETS, 0].astype(jnp.int32)
    padded = (counts + ROW_BLOCK - 1) // ROW_BLOCK * ROW_BLOCK
    pend = jnp.cumsum(padded)
    pstart = pend - padded
    dest = pstart[meta[0]] + meta[1]
    p_rows = t + N_BUCKETS * ROW_BLOCK
    nb = p_rows // ROW_BLOCK
    blk_start = jnp.arange(nb, dtype=jnp.int32) * ROW_BLOCK
    blk_bucket = jnp.minimum(jnp.sum((pend[None, :] <= blk_start[:, None]).astype(jnp.int32), axis=1), N_BUCKETS - 1)
    e_lo, e_hi = _pair_table()
    nvalid = (pend[-1:] // ROW_BLOCK).astype(jnp.int32)

    last_blk = jnp.where(padded > 0, pend // ROW_BLOCK - 1, -1)
    spare_blk = nvalid + jnp.arange(N_BUCKETS, dtype=jnp.int32)
    zero_blocks = jnp.concatenate([last_blk, jnp.where(spare_blk < nb, spare_blk, -1)]).astype(jnp.int32)
    sorted_rows = _dispatch(dest, zero_blocks, tiles, p_rows)
    y_sorted = _experts(e_lo[blk_bucket], e_hi[blk_bucket], nvalid, sorted_rows,
                        w_gate.astype(BF16), w_up.astype(BF16), w_down.astype(BF16))
    return _combine(dest, x1, mod3, norm_final_g.reshape(1, d), y_sorted, s // TM_COMB).reshape(b, s, d)


def kernel(x, c, positions, w_ada, b_ada, norm_mix_g, w_in, sinks, out_norm_sb_g, out_norm_sw_g, w_out, norm_ffn_g,
           w_router_group, w_router_expert, w_gate, w_up, w_down, norm_final_g):
    b, s, d = x.shape
    depth = w_ada.shape[0]
    assert depth == 1, "the final norm is fused into the layer's last stage"
    c_pad = jnp.pad(c, ((0, 8 - b), (0, 0)))
    mod = _adaln(c_pad, w_ada[0], b_ada[0].reshape(1, -1))
    mod3 = mod[:b].reshape(b, 6, d)
    return _layer(x, mod3, positions, norm_mix_g[0], w_in[0], sinks[0], out_norm_sb_g[0], out_norm_sw_g[0],
                  w_out[0], norm_ffn_g[0], w_router_group[0], w_router_expert[0], w_gate[0], w_up[0], w_down[0],
                  norm_final_g)
```

```python
import functools

import jax
import jax.numpy as jnp
from jax import lax
from jax.experimental import pallas as pl
from jax.experimental.pallas import tpu as pltpu

F32 = jnp.float32
BF16 = jnp.bfloat16

HEAD_DIM = 64
N_SB_HEADS = 8
N_SW_HEADS = 8
N_SW_KV_HEADS = 2
WINDOW = 128
ROPE_THETA = 10000.0
N_GROUPS = 4
EXPERTS_PER_GROUP = 4
PAIR_ROLES = ((0, 1), (0, 2), (0, 3), (1, 3), (1, 2), (3, 2))
N_PAIRS = len(PAIR_ROLES)
N_BUCKETS = N_GROUPS * N_PAIRS
EPS = 1e-6

LANES = 128
VMEM_LIMIT = 56 * 1024 * 1024
LOG2_E = 1.4426950408889634
SB_SKIP_LOG2 = -104.0 * LOG2_E

TM_PROJ = 1024
SB_BLOCK = 256
ROW_BLOCK = 256
TM_ROUTE = 512
TM_COMB = 512
SB_DENSE_LEFT = 2
SB_BLOCKS_PER_STEP = 4
SW_BLOCKS_PER_STEP = 4
META_ROWS = 32


def _cparams(sem, vmem=VMEM_LIMIT):
    return pltpu.CompilerParams(dimension_semantics=sem, vmem_limit_bytes=vmem)


def _adaln_kernel(c_ref, w_ref, b_ref, o_ref):
    c = c_ref[...]
    s = c * jax.nn.sigmoid(c)
    o_ref[...] = jnp.dot(s, w_ref[...], precision=lax.Precision.HIGHEST,
                         preferred_element_type=F32) + b_ref[...]


def _adaln(c_pad, w, b):
    rows, d = c_pad.shape
    n = w.shape[1]
    tn = 1024
    return pl.pallas_call(
        _adaln_kernel,
        grid=(n // tn,),
        in_specs=[pl.BlockSpec((rows, d), lambda j: (0, 0)),
                  pl.BlockSpec((d, tn), lambda j: (0, j)),
                  pl.BlockSpec((1, tn), lambda j: (0, j))],
        out_specs=pl.BlockSpec((rows, tn), lambda j: (0, j)),
        out_shape=jax.ShapeDtypeStruct((rows, n), F32),
        compiler_params=_cparams(("parallel",)),
        name="adaln",
    )(c_pad, w, b)


def _rope_kernel(pos_ref, freq_ref, cos_ref, sin_ref):
    ang = pos_ref[...].astype(F32) * freq_ref[...]
    cos_ref[...] = jnp.cos(ang)
    sin_ref[...] = jnp.sin(ang)


def _rope_tables(pos_dense, freq):
    rows = pos_dense.shape[0]
    tm = min(1024, rows)
    row = pl.BlockSpec((tm, LANES), lambda i: (i, 0))
    vec = pl.BlockSpec((1, LANES), lambda i: (0, 0))
    return pl.pallas_call(
        _rope_kernel,
        grid=(rows // tm,),
        in_specs=[row, vec],
        out_specs=[row, row],
        out_shape=[jax.ShapeDtypeStruct((rows, LANES), F32)] * 2,
        compiler_params=_cparams(("parallel",)),
        name="rope",
    )(pos_dense, freq)


def _rms(x):
    return x * lax.rsqrt(jnp.mean(x * x, axis=-1, keepdims=True) + EPS)


def _rotate_half_pairs(x):
    lane = lax.broadcasted_iota(jnp.int32, x.shape, 1)
    first_half = (lane % HEAD_DIM) < (HEAD_DIM // 2)
    return jnp.where(first_half, pltpu.roll(x, LANES - HEAD_DIM // 2, 1), pltpu.roll(x, HEAD_DIM // 2, 1))


def _inproj_kernel(d_sb3, d_swq, x_ref, mod_ref, g_ref, w_ref, cos_ref, sin_ref, sb_ref, q_ref, kv_ref):
    x = x_ref[...]
    shift = mod_ref[0, 0:1, :]
    scale = mod_ref[0, 1:2, :]
    h = (_rms(x) * g_ref[...]) * (1.0 + scale) + shift
    hb = h.astype(BF16)
    chunk = 512
    for c0 in range(0, d_sb3, chunk):
        sb_ref[:, c0:c0 + chunk] = jnp.dot(hb, w_ref[:, c0:c0 + chunk],
                                           preferred_element_type=F32).astype(BF16)
    cos = cos_ref[...]
    sin = sin_ref[...]
    q = jnp.dot(hb, w_ref[:, d_sb3:d_sb3 + d_swq], preferred_element_type=F32)
    for c0 in range(0, d_swq, LANES):
        qt = q[:, c0:c0 + LANES]
        q_ref[:, c0:c0 + LANES] = (qt * cos + _rotate_half_pairs(qt) * sin).astype(BF16)
    kv = jnp.dot(hb, w_ref[:, d_sb3 + d_swq:], preferred_element_type=F32)
    k = kv[:, :LANES]
    kv_ref[:, :LANES] = (k * cos + _rotate_half_pairs(k) * sin).astype(BF16)
    kv_ref[:, LANES:] = kv[:, LANES:].astype(BF16)


def _inproj(x2, mod3, g, w_bf, cos_t, sin_t, tiles_per_batch, d_sb3, d_swq):
    t, d = x2.shape
    d_in = w_bf.shape[1]
    d_kv = d_in - d_sb3 - d_swq
    tm = TM_PROJ
    return pl.pallas_call(
        functools.partial(_inproj_kernel, d_sb3, d_swq),
        grid=(t // tm,),
        in_specs=[pl.BlockSpec((tm, d), lambda i: (i, 0)),
                  pl.BlockSpec((1, 6, d), lambda i: (i // tiles_per_batch, 0, 0)),
                  pl.BlockSpec((1, d), lambda i: (0, 0)),
                  pl.BlockSpec((d, d_in), lambda i: (0, 0)),
                  pl.BlockSpec((tm, LANES), lambda i: (i, 0)),
                  pl.BlockSpec((tm, LANES), lambda i: (i, 0))],
        out_specs=[pl.BlockSpec((tm, d_sb3), lambda i: (i, 0)),
                   pl.BlockSpec((tm, d_swq), lambda i: (i, 0)),
                   pl.BlockSpec((tm, d_kv), lambda i: (i, 0))],
        out_shape=[jax.ShapeDtypeStruct((t, d_sb3), BF16),
                   jax.ShapeDtypeStruct((t, d_swq), BF16),
                   jax.ShapeDtypeStruct((t, d_kv), BF16)],
        compiler_params=_cparams(("parallel",)),
        name="inproj",
    )(x2, mod3, g, w_bf, cos_t, sin_t)


def _dot_nt(a, b):
    return lax.dot_general(a, b, (((1,), (1,)), ((), ())), preferred_element_type=F32)


def _sb_block(qm, k, v, tri, carry, causal):
    z = _dot_nt(qm, k)
    log_beta = jnp.minimum(z, 0.0) - jnp.log2(1.0 + jnp.exp2(-jnp.abs(z)))
    log_rest = log_beta - z
    if causal is not None:
        log_rest = jnp.where(causal, log_rest, 0.0)
    later = jnp.dot(log_rest.astype(BF16), tri, preferred_element_type=F32)
    w = jnp.exp2(log_beta + later + carry)
    if causal is not None:
        w = jnp.where(causal, w, 0.0)
    out = jnp.dot(w.astype(BF16), v, preferred_element_type=F32)
    return out, carry + jnp.sum(log_rest, axis=1, keepdims=True)


def _sb_kernel(blk, n_q, q_ref, k_ref, v_ref, tri_ref, o_ref):
    i = pl.program_id(2)
    tri = tri_ref[...]
    lane = lax.broadcasted_iota(jnp.int32, (blk, LANES), 1)
    row = lax.broadcasted_iota(jnp.int32, (2 * blk, blk), 0)
    col = lax.broadcasted_iota(jnp.int32, (2 * blk, blk), 1)
    causal = col < (row % blk)

    def sweep(qs, kb, carry, causal_mask):
        s = pl.multiple_of(kb * blk, blk)
        return _sb_block(qs, k_ref[0, pl.ds(s, blk), :], v_ref[0, pl.ds(s, blk), :], tri, carry, causal_mask)

    dense = []
    for j in range(n_q):
        g = i * n_q + j
        q = q_ref[0, j * blk:(j + 1) * blk, :]
        qs = jnp.concatenate([jnp.where((lane >= h * HEAD_DIM) & (lane < (h + 1) * HEAD_DIM), q, jnp.zeros_like(q))
                              for h in range(2)], axis=0)
        acc, carry = sweep(qs, g, jnp.zeros((2 * blk, 1), F32), causal)
        for left in range(1, SB_DENSE_LEFT + 1):
            out, carry = sweep(qs, jnp.maximum(g - left, 0), jnp.where(g >= left, carry, -1e30), None)
            acc = acc + out
        dense.append((g, qs, carry, acc))

    for j, (g, qs, carry, acc) in enumerate(dense):
        def cond(state):
            kb, carry, _ = state
            return jnp.logical_and(kb >= 0, jnp.max(carry) >= SB_SKIP_LOG2)

        def body(state, qs=qs):
            kb, carry, acc = state
            out, carry = sweep(qs, kb, carry, None)
            return kb - 1, carry, acc + out

        _, _, acc = lax.while_loop(cond, body, (g - 1 - SB_DENSE_LEFT, carry, acc))
        o_ref[0, j * blk:(j + 1) * blk, :] = jnp.where(lane < HEAD_DIM, acc[:blk], acc[blk:]).astype(o_ref.dtype)


def _sb_attention(qkv3, tri):
    b, s, d3 = qkv3.shape
    d_sb = d3 // 3
    n_pairs = d_sb // LANES
    blk = SB_BLOCK
    n_q = SB_BLOCKS_PER_STEP
    return pl.pallas_call(
        functools.partial(_sb_kernel, blk, n_q),
        grid=(b, n_pairs, s // (blk * n_q)),
        in_specs=[pl.BlockSpec((1, blk * n_q, LANES), lambda bi, p, i: (bi, i, p)),
                  pl.BlockSpec((1, s, LANES), lambda bi, p, i: (bi, 0, n_pairs + p)),
                  pl.BlockSpec((1, s, LANES), lambda bi, p, i: (bi, 0, 2 * n_pairs + p)),
                  pl.BlockSpec((blk, blk), lambda bi, p, i: (0, 0))],
        out_specs=pl.BlockSpec((1, blk * n_q, LANES), lambda bi, p, i: (bi, i, p)),
        out_shape=jax.ShapeDtypeStruct((b, s, d_sb), BF16),
        compiler_params=_cparams(("parallel", "parallel", "arbitrary")),
        name="sb_attn",
    )(qkv3, qkv3, qkv3, tri)


def _sw_kernel(n_tiles, n_sub, sinks_ref, q_ref, kvp_ref, kvc_ref, o_ref):
    i = pl.program_id(1)
    w = WINDOW
    qi = lax.broadcasted_iota(jnp.int32, (w, 2 * w), 0)
    kj = lax.broadcasted_iota(jnp.int32, (w, 2 * w), 1)
    diff = w + qi - kj
    in_window = (diff >= 0) & (diff < w)
    lane = lax.broadcasted_iota(jnp.int32, (w, LANES), 1)
    for j in range(n_sub):
        if j == 0:
            kk = jnp.concatenate([kvp_ref[0, :, :LANES], kvc_ref[0, :w, :LANES]], axis=0)
            vv = jnp.concatenate([kvp_ref[0, :, LANES:], kvc_ref[0, :w, LANES:]], axis=0)
            mask = in_window & ((i * n_sub) * w + kj - w >= 0)
        else:
            kk = kvc_ref[0, (j - 1) * w:(j + 1) * w, :LANES]
            vv = kvc_ref[0, (j - 1) * w:(j + 1) * w, LANES:]
            mask = in_window
        for t in range(n_tiles):
            qt = q_ref[0, j * w:(j + 1) * w, t * LANES:(t + 1) * LANES]
            outs = []
            for half in range(2):
                head = t + half * n_tiles
                in_head = (lane >= half * HEAD_DIM) & (lane < (half + 1) * HEAD_DIM)
                qm = jnp.where(in_head, qt, jnp.zeros_like(qt))
                s = jnp.where(mask, _dot_nt(qm, kk), -1e30)
                sink = sinks_ref[head]
                m = jnp.maximum(jnp.max(s, axis=-1, keepdims=True), sink)
                p = jnp.exp(s - m)
                denom = jnp.sum(p, axis=-1, keepdims=True) + jnp.exp(sink - m)
                outs.append(jnp.dot(p.astype(BF16), vv, preferred_element_type=F32) * (1.0 / denom))
            o_ref[0, j * w:(j + 1) * w, t * LANES:(t + 1) * LANES] = jnp.where(
                lane < HEAD_DIM, outs[0], outs[1]).astype(o_ref.dtype)


def _sw_attention(sinks, q3, kv3):
    b, s, dq = q3.shape
    dkv = kv3.shape[2]
    w = WINDOW
    n_sub = SW_BLOCKS_PER_STEP
    n_tiles = dq // LANES
    return pl.pallas_call(
        functools.partial(_sw_kernel, n_tiles, n_sub),
        grid=(b, s // (w * n_sub)),
        in_specs=[pl.BlockSpec(memory_space=pltpu.SMEM),
                  pl.BlockSpec((1, w * n_sub, dq), lambda bi, i: (bi, i, 0)),
                  pl.BlockSpec((1, w, dkv), lambda bi, i: (bi, jnp.maximum(i * n_sub - 1, 0), 0)),
                  pl.BlockSpec((1, w * n_sub, dkv), lambda bi, i: (bi, i, 0))],
        out_specs=pl.BlockSpec((1, w * n_sub, dq), lambda bi, i: (bi, i, 0)),
        out_shape=jax.ShapeDtypeStruct((b, s, dq), BF16),
        compiler_params=_cparams(("parallel", "parallel")),
        name="sw_attn",
    )(sinks, q3, kv3, kv3)


def _first_max(vals):
    m = vals[0]
    for v in vals[1:]:
        m = jnp.maximum(m, v)
    idx = jnp.full(m.shape, len(vals) - 1, jnp.int32)
    for k in range(len(vals) - 2, -1, -1):
        idx = jnp.where(vals[k] == m, k, idx)
    return m, idx


def _route(lg):
    g = [lg[k:k + 1, :] for k in range(N_GROUPS)]
    gmax, gsel = _first_max(g)
    gsum = g[0] * 0.0
    for k in range(N_GROUPS):
        gsum = gsum + jnp.exp(g[k] - gmax)
    gprob = 1.0 / gsum
    within = []
    for k in range(EXPERTS_PER_GROUP):
        v = lg[N_GROUPS + (N_GROUPS - 1) * EXPERTS_PER_GROUP + k:N_GROUPS + (N_GROUPS - 1) * EXPERTS_PER_GROUP + k + 1, :]
        for grp in range(N_GROUPS - 2, -1, -1):
            r = N_GROUPS + grp * EXPERTS_PER_GROUP + k
            v = jnp.where(gsel == grp, lg[r:r + 1, :], v)
        within.append(v)
    m1, i1 = _first_max(within)
    rest = [jnp.where(i1 == k, -jnp.inf, within[k]) for k in range(EXPERTS_PER_GROUP)]
    m2, i2 = _first_max(rest)
    e2 = jnp.exp(m2 - m1)
    p1 = gprob / (1.0 + e2)
    p2 = gprob * e2 / (1.0 + e2)
    pair = jnp.zeros_like(i1)
    w_a = jnp.zeros_like(p1)
    w_b = jnp.zeros_like(p1)
    for p, (a, b) in enumerate(PAIR_ROLES):
        fwd = (i1 == a) & (i2 == b)
        rev = (i1 == b) & (i2 == a)
        pair = jnp.where(fwd | rev, p, pair)
        w_a = jnp.where(fwd, p1, jnp.where(rev, p2, w_a))
        w_b = jnp.where(fwd, p2, jnp.where(rev, p1, w_b))
    return gsel * N_PAIRS + pair, w_a, w_b


ROW_TILE = 8


def _write_row_tiles(buf, h2, w_rows):
    tm, d = h2.shape
    words_per_row = d // 2
    hi = pltpu.bitcast(h2[:, :words_per_row].astype(BF16).astype(F32), jnp.uint32)
    lo = pltpu.bitcast(h2[:, words_per_row:].astype(BF16).astype(F32), jnp.uint32)
    words = hi | (lo >> 16)
    n_word_rows = words_per_row // LANES
    for s in range(n_word_rows):
        buf[pl.ds(s, tm, stride=ROW_TILE), :] = words[:, s * LANES:(s + 1) * LANES]
    buf[pl.ds(n_word_rows, tm, stride=ROW_TILE), :] = pltpu.bitcast(w_rows.T, jnp.uint32)
    for s in range(n_word_rows + 1, ROW_TILE):
        buf[pl.ds(s, tm, stride=ROW_TILE), :] = jnp.zeros((tm, LANES), jnp.uint32)


def _ffn_input(x1, mod_ref, gffn_ref):
    shift2 = mod_ref[0, 3:4, :]
    scale2 = mod_ref[0, 4:5, :]
    return (_rms(x1) * gffn_ref[...]) * (1.0 + scale2) + shift2


def _outproj_kernel(d_sb, osb_ref, osw_ref, x_ref, mod_ref, gsb_ref, gsw_ref, wo_ref, gffn_ref, wr_ref, tri_ref,
                    x1_ref, meta_ref, wts_ref, cnt_ref, run_ref):
    i = pl.program_id(0)
    tm = x_ref.shape[0]

    @pl.when(i == 0)
    def _():
        run_ref[...] = jnp.zeros_like(run_ref)

    gate1 = mod_ref[0, 2:3, :]
    a = (_rms(osb_ref[...].astype(F32)) * gsb_ref[...]).astype(BF16)
    bsw = (_rms(osw_ref[...].astype(F32)) * gsw_ref[...]).astype(BF16)
    mix = jnp.dot(a, wo_ref[:d_sb, :], preferred_element_type=F32) + jnp.dot(bsw, wo_ref[d_sb:, :],
                                                                                preferred_element_type=F32)
    x1 = x_ref[...] + gate1 * mix
    x1_ref[...] = x1
    h2 = _ffn_input(x1, mod_ref, gffn_ref)
    h2_hi = h2.astype(BF16)
    h2_lo = (h2 - h2_hi.astype(F32)).astype(BF16)
    wr2 = wr_ref[...]
    by_hi = _dot_nt(wr2, h2_hi)
    lg = by_hi[:META_ROWS] + by_hi[META_ROWS:] + _dot_nt(wr2[:META_ROWS], h2_lo)
    bucket, w_a, w_b = _route(lg)
    rows = lax.broadcasted_iota(jnp.int32, (META_ROWS, tm), 0)
    onehot = rows == bucket
    prefix = jnp.dot(jnp.where(onehot, 1.0, 0.0).astype(BF16), tri_ref[...], preferred_element_type=F32)
    run = run_ref[...]
    rank = jnp.sum(jnp.where(onehot, prefix - 1.0 + run, 0.0), axis=0, keepdims=True)
    run = run + jnp.sum(jnp.where(onehot, 1.0, 0.0), axis=1, keepdims=True)
    run_ref[...] = run
    cnt_ref[...] = jnp.broadcast_to(run, cnt_ref.shape)
    meta_ref[...] = jnp.concatenate([bucket, rank.astype(jnp.int32), jnp.zeros((6, tm), jnp.int32)], axis=0)
    wts_ref[...] = jnp.concatenate([w_a, w_b, jnp.zeros((6, tm), F32)], axis=0)


def _outproj(o_sb, o_sw, x2, mod3, g_sb, g_sw, wo_bf, g_ffn, wr, tri, tiles_per_batch):
    t, d = x2.shape
    d_sb = o_sb.shape[1]
    d_sw = o_sw.shape[1]
    tm = TM_ROUTE
    const = lambda shape: pl.BlockSpec(shape, lambda i: tuple(0 for _ in shape))
    return pl.pallas_call(
        functools.partial(_outproj_kernel, d_sb),
        grid=(t // tm,),
        in_specs=[pl.BlockSpec((tm, d_sb), lambda i: (i, 0)),
                  pl.BlockSpec((tm, d_sw), lambda i: (i, 0)),
                  pl.BlockSpec((tm, d), lambda i: (i, 0)),
                  pl.BlockSpec((1, 6, d), lambda i: (i // tiles_per_batch, 0, 0)),
                  const((1, d_sb)), const((1, d_sw)), const((d_sb + d_sw, d)), const((1, d)),
                  const((2 * META_ROWS, d)), const((tm, tm))],
        out_specs=[pl.BlockSpec((tm, d), lambda i: (i, 0)),
                   pl.BlockSpec((8, tm), lambda i: (0, i)),
                   pl.BlockSpec((8, tm), lambda i: (0, i)),
                   const((META_ROWS, LANES))],
        out_shape=[jax.ShapeDtypeStruct((t, d), F32),
                   jax.ShapeDtypeStruct((8, t), jnp.int32),
                   jax.ShapeDtypeStruct((8, t), F32),
                   jax.ShapeDtypeStruct((META_ROWS, LANES), F32)],
        scratch_shapes=[pltpu.VMEM((META_ROWS, 1), F32)],
        compiler_params=_cparams(("arbitrary",)),
        name="outproj",
    )(o_sb, o_sw, x2, mod3, g_sb, g_sw, wo_bf, g_ffn, wr, tri)


def _tile_rows(row):
    start = row * ROW_TILE
    return pl.ds(start if isinstance(row, int) else pl.multiple_of(start, ROW_TILE), ROW_TILE)


def _tile_copy(src_ref, src_row, dst_ref, dst_row, sem):
    return pltpu.make_async_copy(src_ref.at[_tile_rows(src_row), :], dst_ref.at[_tile_rows(dst_row), :], sem)


def _for_rows_with_index(n_rows, read_index, act, group=8, unrolled=False):
    def body(g, c):
        rows = [g * group + u for u in range(group)]
        idx = [read_index(r) for r in rows]
        for r, ix in zip(rows, idx):
            act(r, ix)
        return c
    if unrolled:
        for g in range(n_rows // group):
            body(g, 0)
    else:
        lax.fori_loop(0, n_rows // group, body, 0)


def _wait_tiles(like_ref, hbm_ref, sem):
    n = like_ref.shape[0]
    pltpu.make_async_copy(hbm_ref.at[pl.ds(0, n), :], like_ref, sem).wait()


def _dispatch_kernel(dest_ref, zblk_ref, x1_ref, mod_ref, gffn_ref, wts_ref, out_ref,
                     rows, zero_blk, sems, zero_sem):
    i = pl.program_id(0)
    n = pl.num_programs(0)
    tm = x1_ref.shape[0]
    slot = i % 2
    buf = rows.at[slot]

    @pl.when(i == 0)
    def _():
        zero_blk[...] = jnp.zeros_like(zero_blk)
        blk_rows = zero_blk.shape[0]

        def each_listed_block(act):
            def step(e, c):
                z = zblk_ref[e]

                @pl.when(z >= 0)
                def _():
                    dst = out_ref.at[pl.ds(pl.multiple_of(z * blk_rows, blk_rows), blk_rows), :]
                    act(pltpu.make_async_copy(zero_blk, dst, zero_sem))
                return c
            lax.fori_loop(0, zblk_ref.shape[0], step, 0)

        each_listed_block(lambda cp: cp.start())
        each_listed_block(lambda cp: cp.wait())

    @pl.when(i >= 2)
    def _():
        _wait_tiles(buf, out_ref, sems.at[slot])

    h2 = _ffn_input(x1_ref[...], mod_ref, gffn_ref)
    _write_row_tiles(buf, h2, jnp.concatenate([wts_ref[...], jnp.zeros((LANES - 8, tm), F32)], axis=0))

    def issue(r, dst_row):
        _tile_copy(buf, r, out_ref, dst_row, sems.at[slot]).start()

    _for_rows_with_index(tm, lambda r: dest_ref[i * tm + r], issue)

    @pl.when(i == n - 1)
    def _():
        _wait_tiles(buf, out_ref, sems.at[slot])

        @pl.when(n >= 2)
        def _():
            _wait_tiles(rows.at[1 - slot], out_ref, sems.at[1 - slot])


def _dispatch(dest, zero_blocks, x1, mod3, g_ffn, wts, p_rows, tiles_per_batch):
    t, d = x1.shape
    tm = TM_ROUTE
    return pl.pallas_call(
        _dispatch_kernel,
        grid_spec=pltpu.PrefetchScalarGridSpec(
            num_scalar_prefetch=2,
            grid=(t // tm,),
            in_specs=[pl.BlockSpec((tm, d), lambda i, *_: (i, 0)),
                      pl.BlockSpec((1, 6, d), lambda i, *_: (i // tiles_per_batch, 0, 0)),
                      pl.BlockSpec((1, d), lambda i, *_: (0, 0)),
                      pl.BlockSpec((8, tm), lambda i, *_: (0, i))],
            out_specs=pl.BlockSpec(memory_space=pl.ANY),
            scratch_shapes=[pltpu.VMEM((2, tm * ROW_TILE, LANES), jnp.uint32),
                            pltpu.VMEM((ROW_BLOCK * ROW_TILE, LANES), jnp.uint32),
                            pltpu.SemaphoreType.DMA((2,)), pltpu.SemaphoreType.DMA(())]),
        out_shape=jax.ShapeDtypeStruct((p_rows * ROW_TILE, LANES), jnp.uint32),
        compiler_params=_cparams(("arbitrary",)),
        name="dispatch",
    )(dest, zero_blocks, x1, mod3, g_ffn, wts)


def _experts_kernel(d, ea_ref, eb_ref, nvalid_ref, x_ref, wga_ref, wua_ref, wda_ref, wgb_ref, wub_ref, wdb_ref, y_ref):
    j = pl.program_id(0)
    rb = x_ref.shape[0] // ROW_TILE
    n_word_rows = d // 2 // LANES

    @pl.when(j < nvalid_ref[0])
    def _():
        high, low = [], []
        for s in range(n_word_rows):
            w = x_ref[pl.ds(s, rb, stride=ROW_TILE), :]
            high.append(pltpu.bitcast(w & jnp.uint32(0xFFFF0000), F32).astype(BF16))
            low.append(pltpu.bitcast(w << 16, F32).astype(BF16))
        xb = jnp.concatenate(high + low, axis=1)
        wts = pltpu.bitcast(x_ref[pl.ds(n_word_rows, rb, stride=ROW_TILE), :], F32)

        def hidden(wg_ref, wu_ref, wrow):
            gt = jnp.dot(xb, wg_ref[0], preferred_element_type=F32)
            up = jnp.dot(xb, wu_ref[0], preferred_element_type=F32)
            return ((gt * jax.nn.sigmoid(gt)) * up * wrow).astype(BF16)

        ha = hidden(wga_ref, wua_ref, wts[:, 0:1])
        hb = hidden(wgb_ref, wub_ref, wts[:, 1:2])
        y = (jnp.dot(ha, wda_ref[0], preferred_element_type=F32)
             + jnp.dot(hb, wdb_ref[0], preferred_element_type=F32))
        for c in range(ROW_TILE):
            y_ref[pl.ds(c, rb, stride=ROW_TILE), :] = y[:, c * LANES:(c + 1) * LANES]

    @pl.when(j >= nvalid_ref[0])
    def _():
        y_ref[...] = jnp.zeros_like(y_ref)


def _experts(blk_ea, blk_eb, nvalid, sorted_rows, wg_bf, wu_bf, wd_bf):
    n_e, d, d_e = wg_bf.shape
    assert d == ROW_TILE * LANES, "one expert output row must fill one (8, 128) f32 tile"
    rb = ROW_BLOCK
    nb = sorted_rows.shape[0] // (rb * ROW_TILE)
    up_a = pl.BlockSpec((1, d, d_e), lambda j, ea, eb, nv: (ea[j], 0, 0))
    up_b = pl.BlockSpec((1, d, d_e), lambda j, ea, eb, nv: (eb[j], 0, 0))
    dn_a = pl.BlockSpec((1, d_e, d), lambda j, ea, eb, nv: (ea[j], 0, 0))
    dn_b = pl.BlockSpec((1, d_e, d), lambda j, ea, eb, nv: (eb[j], 0, 0))
    rows_in = rows_out = pl.BlockSpec((rb * ROW_TILE, LANES), lambda j, ea, eb, nv: (j, 0))
    return pl.pallas_call(
        functools.partial(_experts_kernel, d),
        grid_spec=pltpu.PrefetchScalarGridSpec(
            num_scalar_prefetch=3,
            grid=(nb,),
            in_specs=[rows_in, up_a, up_a, dn_a, up_b, up_b, dn_b],
            out_specs=rows_out),
        out_shape=jax.ShapeDtypeStruct(sorted_rows.shape, F32),
        compiler_params=_cparams(("arbitrary",)),
        name="experts",
    )(blk_ea, blk_eb, nvalid, sorted_rows, wg_bf, wu_bf, wd_bf, wg_bf, wu_bf, wd_bf)


def _combine_kernel(dest_ref, x1_ref, mod_ref, g_ref, y_hbm, o_ref, ybuf, sems):
    i = pl.program_id(0)
    n = pl.num_programs(0)
    tm = x1_ref.shape[0]
    slot = i % 2

    def issue(tile, slot_, unrolled):
        def one(r, src_row):
            _tile_copy(y_hbm, src_row, ybuf.at[slot_], r, sems.at[slot_]).start()
        _for_rows_with_index(tm, lambda r: dest_ref[tile * tm + r], one, unrolled=unrolled)

    @pl.when(i == 0)
    def _():
        issue(0, 0, False)

    _wait_tiles(ybuf.at[slot], y_hbm, sems.at[slot])
    issue(jnp.minimum(i + 1, n - 1), 1 - slot, True)
    y = jnp.concatenate([ybuf[slot, pl.ds(c, tm, stride=ROW_TILE), :] for c in range(ROW_TILE)], axis=1)
    gate2 = mod_ref[0, 5:6, :]
    o_ref[...] = _rms(x1_ref[...] + gate2 * y) * g_ref[...]

    @pl.when(i == n - 1)
    def _():
        _wait_tiles(ybuf.at[1 - slot], y_hbm, sems.at[1 - slot])


def _combine(dest, x1, mod3, g_final, y_sorted, tiles_per_batch):
    t, d = x1.shape
    tm = TM_COMB
    return pl.pallas_call(
        _combine_kernel,
        grid_spec=pltpu.PrefetchScalarGridSpec(
            num_scalar_prefetch=1,
            grid=(t // tm,),
            in_specs=[pl.BlockSpec((tm, d), lambda i, dest: (i, 0)),
                      pl.BlockSpec((1, 6, d), lambda i, dest: (i // tiles_per_batch, 0, 0)),
                      pl.BlockSpec((1, d), lambda i, dest: (0, 0)),
                      pl.BlockSpec(memory_space=pl.ANY)],
            out_specs=pl.BlockSpec((tm, d), lambda i, dest: (i, 0)),
            scratch_shapes=[pltpu.VMEM((2, tm * ROW_TILE, LANES), F32), pltpu.SemaphoreType.DMA((2,))]),
        out_shape=jax.ShapeDtypeStruct((t, d), F32),
        compiler_params=_cparams(("arbitrary",)),
        name="combine",
    )(dest, x1, mod3, g_final, y_sorted)


def _pair_table():
    ea = [g * EXPERTS_PER_GROUP + a for g in range(N_GROUPS) for a, _ in PAIR_ROLES]
    eb = [g * EXPERTS_PER_GROUP + b for g in range(N_GROUPS) for _, b in PAIR_ROLES]
    return jnp.array(ea, jnp.int32), jnp.array(eb, jnp.int32)


def _layer(x, mod3, positions, norm_mix_g, w_in, sinks, out_norm_sb_g, out_norm_sw_g, w_out, norm_ffn_g,
           w_router_group, w_router_expert, w_gate, w_up, w_down, norm_final_g):
    b, s, d = x.shape
    t = b * s
    d_sb = N_SB_HEADS * HEAD_DIM
    d_sw = N_SW_HEADS * HEAD_DIM
    d_kv = N_SW_KV_HEADS * HEAD_DIM
    scale = HEAD_DIM ** -0.5
    x2 = x.reshape(t, d)

    n_tiles = d_sw // LANES
    sw_heads = jnp.array([h for j in range(n_tiles) for h in (j, j + n_tiles)], jnp.int32)
    sw_cols = (sw_heads[:, None] * HEAD_DIM + jnp.arange(HEAD_DIM, dtype=jnp.int32)[None, :]).reshape(-1)
    q_sw_cols = 3 * d_sb + sw_cols
    w_bf = jnp.concatenate([w_in[:, :d_sb] * (scale * LOG2_E), w_in[:, d_sb:3 * d_sb],
                            w_in[:, q_sw_cols] * scale, w_in[:, 3 * d_sb + d_sw:]], axis=1).astype(BF16)

    inv_freq = ROPE_THETA ** (-jnp.arange(0, HEAD_DIM, 2, dtype=F32) / HEAD_DIM)
    freq = jnp.tile(inv_freq, LANES // (HEAD_DIM // 2)).reshape(1, LANES)
    sign = jnp.tile(jnp.concatenate([-jnp.ones(HEAD_DIM // 2, F32), jnp.ones(HEAD_DIM // 2, F32)]),
                    LANES // HEAD_DIM).reshape(1, LANES)
    n_freq = HEAD_DIM // 2
    per_row = LANES // n_freq
    pos_dense = jnp.repeat(positions.reshape(t // per_row, per_row), n_freq, axis=1)
    cos_d, sin_d = _rope_tables(pos_dense, freq)
    cos_t = jnp.tile(cos_d.reshape(t, n_freq), (1, per_row))
    sin_t = jnp.tile(sin_d.reshape(t, n_freq), (1, per_row)) * sign

    qkv_sb, q_sw, kv_sw = _inproj(x2, mod3, norm_mix_g.reshape(1, d), w_bf, cos_t, sin_t, s // TM_PROJ,
                                  3 * d_sb, d_sw)

    tri_sb = (jnp.arange(SB_BLOCK)[:, None] > jnp.arange(SB_BLOCK)[None, :]).astype(BF16)
    o_sb = _sb_attention(qkv_sb.reshape(b, s, 3 * d_sb), tri_sb).reshape(t, d_sb)
    o_sw = _sw_attention(sinks, q_sw.reshape(b, s, d_sw), kv_sw.reshape(b, s, 2 * d_kv)).reshape(t, d_sw)

    wo_bf = jnp.concatenate([w_out[:d_sb], w_out[d_sb + sw_cols]], axis=0).astype(BF16)
    wr = jnp.concatenate([w_router_group.T, w_router_expert.T,
                          jnp.zeros((META_ROWS - N_GROUPS - N_GROUPS * EXPERTS_PER_GROUP, d), F32)], axis=0)
    wr_hi = wr.astype(BF16)
    wr = jnp.concatenate([wr_hi, (wr - wr_hi.astype(F32)).astype(BF16)], axis=0)
    tri_rt = (jnp.arange(TM_ROUTE)[:, None] <= jnp.arange(TM_ROUTE)[None, :]).astype(BF16)
    g_ffn = norm_ffn_g.reshape(1, d)
    x1, meta, wts, cnt = _outproj(o_sb, o_sw, x2, mod3, out_norm_sb_g.reshape(1, d_sb),
                                  out_norm_sw_g[sw_cols].reshape(1, d_sw), wo_bf, g_ffn, wr, tri_rt, s // TM_ROUTE)

    counts = cnt[:N_BUCKETS, 0].astype(jnp.int32)
    padded = (counts + ROW_BLOCK - 1) // ROW_BLOCK * ROW_BLOCK
    pend = jnp.cumsum(padded)
    pstart = pend - padded
    dest = pstart[meta[0]] + meta[1]
    p_rows = t + N_BUCKETS * ROW_BLOCK
    nb = p_rows // ROW_BLOCK
    blk_start = jnp.arange(nb, dtype=jnp.int32) * ROW_BLOCK
    blk_bucket = jnp.minimum(jnp.sum((pend[None, :] <= blk_start[:, None]).astype(jnp.int32), axis=1), N_BUCKETS - 1)
    e_lo, e_hi = _pair_table()
    nvalid = (pend[-1:] // ROW_BLOCK).astype(jnp.int32)

    last_blk = jnp.where(padded > 0, pend // ROW_BLOCK - 1, -1)
    spare_blk = nvalid + jnp.arange(N_BUCKETS, dtype=jnp.int32)
    zero_blocks = jnp.concatenate([last_blk, jnp.where(spare_blk < nb, spare_blk, -1)]).astype(jnp.int32)
    sorted_rows = _dispatch(dest, zero_blocks, x1, mod3, g_ffn, wts, p_rows, s // TM_ROUTE)
    y_sorted = _experts(e_lo[blk_bucket], e_hi[blk_bucket], nvalid, sorted_rows,
                        w_gate.astype(BF16), w_up.astype(BF16), w_down.astype(BF16))
    return _combine(dest, x1, mod3, norm_final_g.reshape(1, d), y_sorted, s // TM_COMB).reshape(b, s, d)


def kernel(x, c, positions, w_ada, b_ada, norm_mix_g, w_in, sinks, out_norm_sb_g, out_norm_sw_g, w_out, norm_ffn_g,
           w_router_group, w_router_expert, w_gate, w_up, w_down, norm_final_g):
    b, s, d = x.shape
    depth = w_ada.shape[0]
    assert depth == 1, "the final norm is fused into the layer's last stage"
    c_pad = jnp.pad(c, ((0, 8 - b), (0, 0)))
    mod = _adaln(c_pad, w_ada[0], b_ada[0].reshape(1, -1))
    mod3 = mod[:b].reshape(b, 6, d)
    return _layer(x, mod3, positions, norm_mix_g[0], w_in[0], sinks[0], out_norm_sb_g[0], out_norm_sw_g[0],
                  w_out[0], norm_ffn_g[0], w_router_group[0], w_router_expert[0], w_gate[0], w_up[0], w_down[0],
                  norm_final_g)
```

```python
import functools

import jax
import jax.numpy as jnp
from jax import lax
from jax.experimental import pallas as pl
from jax.experimental.pallas import tpu as pltpu

F32 = jnp.float32
BF16 = jnp.bfloat16

HEAD_DIM = 64
N_SB_HEADS = 8
N_SW_HEADS = 8
N_SW_KV_HEADS = 2
WINDOW = 128
ROPE_THETA = 10000.0
N_GROUPS = 4
EXPERTS_PER_GROUP = 4
PAIR_ROLES = ((0, 1), (0, 2), (0, 3), (1, 3), (1, 2), (3, 2))
N_PAIRS = len(PAIR_ROLES)
N_BUCKETS = N_GROUPS * N_PAIRS
EPS = 1e-6

LANES = 128
VMEM_LIMIT = 56 * 1024 * 1024
LOG2_E = 1.4426950408889634
SB_SKIP_LOG2 = -104.0 * LOG2_E

TM_PROJ = 1024
SB_BLOCK = 256
ROW_BLOCK = 256
TM_ROUTE = 512
TM_COMB = 512
SB_DENSE_LEFT = 2
SB_BLOCKS_PER_STEP = 4
SW_BLOCKS_PER_STEP = 4
META_ROWS = 32


def _cparams(sem, vmem=VMEM_LIMIT):
    return pltpu.CompilerParams(dimension_semantics=sem, vmem_limit_bytes=vmem)


def _adaln_kernel(c_ref, w_ref, b_ref, o_ref):
    c = c_ref[...]
    s = c * jax.nn.sigmoid(c)
    o_ref[...] = jnp.dot(s, w_ref[...], precision=lax.Precision.HIGHEST,
                         preferred_element_type=F32) + b_ref[...]


def _adaln(c_pad, w, b):
    rows, d = c_pad.shape
    n = w.shape[1]
    tn = 1024
    return pl.pallas_call(
        _adaln_kernel,
        grid=(n // tn,),
        in_specs=[pl.BlockSpec((rows, d), lambda j: (0, 0)),
                  pl.BlockSpec((d, tn), lambda j: (0, j)),
                  pl.BlockSpec((1, tn), lambda j: (0, j))],
        out_specs=pl.BlockSpec((rows, tn), lambda j: (0, j)),
        out_shape=jax.ShapeDtypeStruct((rows, n), F32),
        compiler_params=_cparams(("parallel",)),
        name="adaln",
    )(c_pad, w, b)


def _rope_kernel(pos_ref, freq_ref, cos_ref, sin_ref):
    ang = pos_ref[...].astype(F32) * freq_ref[...]
    cos_ref[...] = jnp.cos(ang)
    sin_ref[...] = jnp.sin(ang)


def _rope_tables(pos_dense, freq):
    rows = pos_dense.shape[0]
    tm = min(1024, rows)
    row = pl.BlockSpec((tm, LANES), lambda i: (i, 0))
    vec = pl.BlockSpec((1, LANES), lambda i: (0, 0))
    return pl.pallas_call(
        _rope_kernel,
        grid=(rows // tm,),
        in_specs=[row, vec],
        out_specs=[row, row],
        out_shape=[jax.ShapeDtypeStruct((rows, LANES), F32)] * 2,
        compiler_params=_cparams(("parallel",)),
        name="rope",
    )(pos_dense, freq)


def _rms(x):
    return x * lax.rsqrt(jnp.mean(x * x, axis=-1, keepdims=True) + EPS)


def _rotate_half_pairs(x):
    lane = lax.broadcasted_iota(jnp.int32, x.shape, 1)
    first_half = (lane % HEAD_DIM) < (HEAD_DIM // 2)
    return jnp.where(first_half, pltpu.roll(x, LANES - HEAD_DIM // 2, 1), pltpu.roll(x, HEAD_DIM // 2, 1))


def _inproj_kernel(d_sb3, d_swq, x_ref, mod_ref, g_ref, w_ref, cos_ref, sin_ref, sb_ref, q_ref, kv_ref):
    x = x_ref[...]
    shift = mod_ref[0, 0:1, :]
    scale = mod_ref[0, 1:2, :]
    h = (_rms(x) * g_ref[...]) * (1.0 + scale) + shift
    hb = h.astype(BF16)
    chunk = 512
    for c0 in range(0, d_sb3, chunk):
        sb_ref[:, c0:c0 + chunk] = jnp.dot(hb, w_ref[:, c0:c0 + chunk],
                                           preferred_element_type=F32).astype(BF16)
    cos = cos_ref[...]
    sin = sin_ref[...]
    q = jnp.dot(hb, w_ref[:, d_sb3:d_sb3 + d_swq], preferred_element_type=F32)
    for c0 in range(0, d_swq, LANES):
        qt = q[:, c0:c0 + LANES]
        q_ref[:, c0:c0 + LANES] = (qt * cos + _rotate_half_pairs(qt) * sin).astype(BF16)
    kv = jnp.dot(hb, w_ref[:, d_sb3 + d_swq:], preferred_element_type=F32)
    k = kv[:, :LANES]
    kv_ref[:, :LANES] = (k * cos + _rotate_half_pairs(k) * sin).astype(BF16)
    kv_ref[:, LANES:] = kv[:, LANES:].astype(BF16)


def _inproj(x2, mod3, g, w_bf, cos_t, sin_t, tiles_per_batch, d_sb3, d_swq):
    t, d = x2.shape
    d_in = w_bf.shape[1]
    d_kv = d_in - d_sb3 - d_swq
    tm = TM_PROJ
    return pl.pallas_call(
        functools.partial(_inproj_kernel, d_sb3, d_swq),
        grid=(t // tm,),
        in_specs=[pl.BlockSpec((tm, d), lambda i: (i, 0)),
                  pl.BlockSpec((1, 6, d), lambda i: (i // tiles_per_batch, 0, 0)),
                  pl.BlockSpec((1, d), lambda i: (0, 0)),
                  pl.BlockSpec((d, d_in), lambda i: (0, 0)),
                  pl.BlockSpec((tm, LANES), lambda i: (i, 0)),
                  pl.BlockSpec((tm, LANES), lambda i: (i, 0))],
        out_specs=[pl.BlockSpec((tm, d_sb3), lambda i: (i, 0)),
                   pl.BlockSpec((tm, d_swq), lambda i: (i, 0)),
                   pl.BlockSpec((tm, d_kv), lambda i: (i, 0))],
        out_shape=[jax.ShapeDtypeStruct((t, d_sb3), BF16),
                   jax.ShapeDtypeStruct((t, d_swq), BF16),
                   jax.ShapeDtypeStruct((t, d_kv), BF16)],
        compiler_params=_cparams(("parallel",)),
        name="inproj",
    )(x2, mod3, g, w_bf, cos_t, sin_t)


def _dot_nt(a, b):
    return lax.dot_general(a, b, (((1,), (1,)), ((), ())), preferred_element_type=F32)


def _sb_block(qm, k, v, tri, carry, causal):
    z = _dot_nt(qm, k)
    log_beta = jnp.minimum(z, 0.0) - jnp.log2(1.0 + jnp.exp2(-jnp.abs(z)))
    log_rest = log_beta - z
    if causal is not None:
        log_rest = jnp.where(causal, log_rest, 0.0)
    later = jnp.dot(log_rest.astype(BF16), tri, preferred_element_type=F32)
    w = jnp.exp2(log_beta + later + carry)
    if causal is not None:
        w = jnp.where(causal, w, 0.0)
    out = jnp.dot(w.astype(BF16), v, preferred_element_type=F32)
    return out, carry + jnp.sum(log_rest, axis=1, keepdims=True)


def _sb_kernel(blk, n_q, q_ref, k_ref, v_ref, tri_ref, o_ref):
    i = pl.program_id(2)
    tri = tri_ref[...]
    lane = lax.broadcasted_iota(jnp.int32, (blk, LANES), 1)
    row = lax.broadcasted_iota(jnp.int32, (2 * blk, blk), 0)
    col = lax.broadcasted_iota(jnp.int32, (2 * blk, blk), 1)
    causal = col < (row % blk)

    def sweep(qs, kb, carry, causal_mask):
        s = pl.multiple_of(kb * blk, blk)
        return _sb_block(qs, k_ref[0, pl.ds(s, blk), :], v_ref[0, pl.ds(s, blk), :], tri, carry, causal_mask)

    dense = []
    for j in range(n_q):
        g = i * n_q + j
        q = q_ref[0, j * blk:(j + 1) * blk, :]
        qs = jnp.concatenate([jnp.where((lane >= h * HEAD_DIM) & (lane < (h + 1) * HEAD_DIM), q, jnp.zeros_like(q))
                              for h in range(2)], axis=0)
        acc, carry = sweep(qs, g, jnp.zeros((2 * blk, 1), F32), causal)
        for left in range(1, SB_DENSE_LEFT + 1):
            out, carry = sweep(qs, jnp.maximum(g - left, 0), jnp.where(g >= left, carry, -1e30), None)
            acc = acc + out
        dense.append((g, qs, carry, acc))

    for j, (g, qs, carry, acc) in enumerate(dense):
        def cond(state):
            kb, carry, _ = state
            return jnp.logical_and(kb >= 0, jnp.max(carry) >= SB_SKIP_LOG2)

        def body(state, qs=qs):
            kb, carry, acc = state
            out, carry = sweep(qs, kb, carry, None)
            return kb - 1, carry, acc + out

        _, _, acc = lax.while_loop(cond, body, (g - 1 - SB_DENSE_LEFT, carry, acc))
        o_ref[0, j * blk:(j + 1) * blk, :] = jnp.where(lane < HEAD_DIM, acc[:blk], acc[blk:]).astype(o_ref.dtype)


def _sb_attention(qkv3, tri):
    b, s, d3 = qkv3.shape
    d_sb = d3 // 3
    n_pairs = d_sb // LANES
    blk = SB_BLOCK
    n_q = SB_BLOCKS_PER_STEP
    return pl.pallas_call(
        functools.partial(_sb_kernel, blk, n_q),
        grid=(b, n_pairs, s // (blk * n_q)),
        in_specs=[pl.BlockSpec((1, blk * n_q, LANES), lambda bi, p, i: (bi, i, p)),
                  pl.BlockSpec((1, s, LANES), lambda bi, p, i: (bi, 0, n_pairs + p)),
                  pl.BlockSpec((1, s, LANES), lambda bi, p, i: (bi, 0, 2 * n_pairs + p)),
                  pl.BlockSpec((blk, blk), lambda bi, p, i: (0, 0))],
        out_specs=pl.BlockSpec((1, blk * n_q, LANES), lambda bi, p, i: (bi, i, p)),
        out_shape=jax.ShapeDtypeStruct((b, s, d_sb), BF16),
        compiler_params=_cparams(("parallel", "parallel", "arbitrary")),
        name="sb_attn",
    )(qkv3, qkv3, qkv3, tri)


def _sw_kernel(n_tiles, n_sub, sinks_ref, q_ref, kvp_ref, kvc_ref, o_ref):
    i = pl.program_id(1)
    w = WINDOW
    qi = lax.broadcasted_iota(jnp.int32, (w, 2 * w), 0)
    kj = lax.broadcasted_iota(jnp.int32, (w, 2 * w), 1)
    diff = w + qi - kj
    in_window = (diff >= 0) & (diff < w)
    lane = lax.broadcasted_iota(jnp.int32, (w, LANES), 1)
    for j in range(n_sub):
        if j == 0:
            kk = jnp.concatenate([kvp_ref[0, :, :LANES], kvc_ref[0, :w, :LANES]], axis=0)
            vv = jnp.concatenate([kvp_ref[0, :, LANES:], kvc_ref[0, :w, LANES:]], axis=0)
            mask = in_window & ((i * n_sub) * w + kj - w >= 0)
        else:
            kk = kvc_ref[0, (j - 1) * w:(j + 1) * w, :LANES]
            vv = kvc_ref[0, (j - 1) * w:(j + 1) * w, LANES:]
            mask = in_window
        for t in range(n_tiles):
            qt = q_ref[0, j * w:(j + 1) * w, t * LANES:(t + 1) * LANES]
            outs = []
            for half in range(2):
                head = t + half * n_tiles
                in_head = (lane >= half * HEAD_DIM) & (lane < (half + 1) * HEAD_DIM)
                qm = jnp.where(in_head, qt, jnp.zeros_like(qt))
                s = jnp.where(mask, _dot_nt(qm, kk), -1e30)
                sink = sinks_ref[head]
                m = jnp.maximum(jnp.max(s, axis=-1, keepdims=True), sink)
                p = jnp.exp(s - m)
                denom = jnp.sum(p, axis=-1, keepdims=True) + jnp.exp(sink - m)
                outs.append(jnp.dot(p.astype(BF16), vv, preferred_element_type=F32) * (1.0 / denom))
            o_ref[0, j * w:(j + 1) * w, t * LANES:(t + 1) * LANES] = jnp.where(
                lane < HEAD_DIM, outs[0], outs[1]).astype(o_ref.dtype)


def _sw_attention(sinks, q3, kv3):
    b, s, dq = q3.shape
    dkv = kv3.shape[2]
    w = WINDOW
    n_sub = SW_BLOCKS_PER_STEP
    n_tiles = dq // LANES
    return pl.pallas_call(
        functools.partial(_sw_kernel, n_tiles, n_sub),
        grid=(b, s // (w * n_sub)),
        in_specs=[pl.BlockSpec(memory_space=pltpu.SMEM),
                  pl.BlockSpec((1, w * n_sub, dq), lambda bi, i: (bi, i, 0)),
                  pl.BlockSpec((1, w, dkv), lambda bi, i: (bi, jnp.maximum(i * n_sub - 1, 0), 0)),
                  pl.BlockSpec((1, w * n_sub, dkv), lambda bi, i: (bi, i, 0))],
        out_specs=pl.BlockSpec((1, w * n_sub, dq), lambda bi, i: (bi, i, 0)),
        out_shape=jax.ShapeDtypeStruct((b, s, dq), BF16),
        compiler_params=_cparams(("parallel", "parallel")),
        name="sw_attn",
    )(sinks, q3, kv3, kv3)


def _first_max(vals):
    m = vals[0]
    for v in vals[1:]:
        m = jnp.maximum(m, v)
    idx = jnp.full(m.shape, len(vals) - 1, jnp.int32)
    for k in range(len(vals) - 2, -1, -1):
        idx = jnp.where(vals[k] == m, k, idx)
    return m, idx


def _route(lg):
    g = [lg[k:k + 1, :] for k in range(N_GROUPS)]
    gmax, gsel = _first_max(g)
    gsum = g[0] * 0.0
    for k in range(N_GROUPS):
        gsum = gsum + jnp.exp(g[k] - gmax)
    gprob = 1.0 / gsum
    within = []
    for k in range(EXPERTS_PER_GROUP):
        v = lg[N_GROUPS + (N_GROUPS - 1) * EXPERTS_PER_GROUP + k:N_GROUPS + (N_GROUPS - 1) * EXPERTS_PER_GROUP + k + 1, :]
        for grp in range(N_GROUPS - 2, -1, -1):
            r = N_GROUPS + grp * EXPERTS_PER_GROUP + k
            v = jnp.where(gsel == grp, lg[r:r + 1, :], v)
        within.append(v)
    m1, i1 = _first_max(within)
    rest = [jnp.where(i1 == k, -jnp.inf, within[k]) for k in range(EXPERTS_PER_GROUP)]
    m2, i2 = _first_max(rest)
    e2 = jnp.exp(m2 - m1)
    p1 = gprob / (1.0 + e2)
    p2 = gprob * e2 / (1.0 + e2)
    pair = jnp.zeros_like(i1)
    w_a = jnp.zeros_like(p1)
    w_b = jnp.zeros_like(p1)
    for p, (a, b) in enumerate(PAIR_ROLES):
        fwd = (i1 == a) & (i2 == b)
        rev = (i1 == b) & (i2 == a)
        pair = jnp.where(fwd | rev, p, pair)
        w_a = jnp.where(fwd, p1, jnp.where(rev, p2, w_a))
        w_b = jnp.where(fwd, p2, jnp.where(rev, p1, w_b))
    return gsel * N_PAIRS + pair, w_a, w_b


ROW_TILE = 8


def _write_row_tiles(buf, h2, w_rows):
    tm, d = h2.shape
    words_per_row = d // 2
    hi = pltpu.bitcast(h2[:, :words_per_row].astype(BF16).astype(F32), jnp.uint32)
    lo = pltpu.bitcast(h2[:, words_per_row:].astype(BF16).astype(F32), jnp.uint32)
    words = hi | (lo >> 16)
    n_word_rows = words_per_row // LANES
    for s in range(n_word_rows):
        buf[pl.ds(s, tm, stride=ROW_TILE), :] = words[:, s * LANES:(s + 1) * LANES]
    buf[pl.ds(n_word_rows, tm, stride=ROW_TILE), :] = pltpu.bitcast(w_rows.T, jnp.uint32)
    for s in range(n_word_rows + 1, ROW_TILE):
        buf[pl.ds(s, tm, stride=ROW_TILE), :] = jnp.zeros((tm, LANES), jnp.uint32)


def _ffn_input(x1, mod_ref, gffn_ref):
    shift2 = mod_ref[0, 3:4, :]
    scale2 = mod_ref[0, 4:5, :]
    return (_rms(x1) * gffn_ref[...]) * (1.0 + scale2) + shift2


def _outproj_kernel(d_sb, osb_ref, osw_ref, x_ref, mod_ref, gsb_ref, gsw_ref, wo_ref, gffn_ref, wr_ref, tri_ref,
                    x1_ref, meta_ref, wts_ref, cnt_ref, run_ref):
    i = pl.program_id(0)
    tm = x_ref.shape[0]

    @pl.when(i == 0)
    def _():
        run_ref[...] = jnp.zeros_like(run_ref)

    gate1 = mod_ref[0, 2:3, :]
    a = (_rms(osb_ref[...].astype(F32)) * gsb_ref[...]).astype(BF16)
    bsw = (_rms(osw_ref[...].astype(F32)) * gsw_ref[...]).astype(BF16)
    mix = jnp.dot(a, wo_ref[:d_sb, :], preferred_element_type=F32) + jnp.dot(bsw, wo_ref[d_sb:, :],
                                                                                preferred_element_type=F32)
    x1 = x_ref[...] + gate1 * mix
    x1_ref[...] = x1
    h2 = _ffn_input(x1, mod_ref, gffn_ref)
    h2_hi = h2.astype(BF16)
    h2_lo = (h2 - h2_hi.astype(F32)).astype(BF16)
    wr2 = wr_ref[...]
    by_hi = _dot_nt(wr2, h2_hi)
    lg = by_hi[:META_ROWS] + by_hi[META_ROWS:] + _dot_nt(wr2[:META_ROWS], h2_lo)
    bucket, w_a, w_b = _route(lg)
    rows = lax.broadcasted_iota(jnp.int32, (META_ROWS, tm), 0)
    onehot = rows == bucket
    prefix = jnp.dot(jnp.where(onehot, 1.0, 0.0).astype(BF16), tri_ref[...], preferred_element_type=F32)
    run = run_ref[...]
    rank = jnp.sum(jnp.where(onehot, prefix - 1.0 + run, 0.0), axis=0, keepdims=True)
    run = run + jnp.sum(jnp.where(onehot, 1.0, 0.0), axis=1, keepdims=True)
    run_ref[...] = run
    cnt_ref[...] = jnp.broadcast_to(run, cnt_ref.shape)
    meta_ref[...] = jnp.concatenate([bucket, rank.astype(jnp.int32), jnp.zeros((6, tm), jnp.int32)], axis=0)
    wts_ref[...] = jnp.concatenate([w_a, w_b, jnp.zeros((6, tm), F32)], axis=0)


def _outproj(o_sb, o_sw, x2, mod3, g_sb, g_sw, wo_bf, g_ffn, wr, tri, tiles_per_batch):
    t, d = x2.shape
    d_sb = o_sb.shape[1]
    d_sw = o_sw.shape[1]
    tm = TM_ROUTE
    const = lambda shape: pl.BlockSpec(shape, lambda i: tuple(0 for _ in shape))
    return pl.pallas_call(
        functools.partial(_outproj_kernel, d_sb),
        grid=(t // tm,),
        in_specs=[pl.BlockSpec((tm, d_sb), lambda i: (i, 0)),
                  pl.BlockSpec((tm, d_sw), lambda i: (i, 0)),
                  pl.BlockSpec((tm, d), lambda i: (i, 0)),
                  pl.BlockSpec((1, 6, d), lambda i: (i // tiles_per_batch, 0, 0)),
                  const((1, d_sb)), const((1, d_sw)), const((d_sb + d_sw, d)), const((1, d)),
                  const((2 * META_ROWS, d)), const((tm, tm))],
        out_specs=[pl.BlockSpec((tm, d), lambda i: (i, 0)),
                   pl.BlockSpec((8, tm), lambda i: (0, i)),
                   pl.BlockSpec((8, tm), lambda i: (0, i)),
                   const((META_ROWS, LANES))],
        out_shape=[jax.ShapeDtypeStruct((t, d), F32),
                   jax.ShapeDtypeStruct((8, t), jnp.int32),
                   jax.ShapeDtypeStruct((8, t), F32),
                   jax.ShapeDtypeStruct((META_ROWS, LANES), F32)],
        scratch_shapes=[pltpu.VMEM((META_ROWS, 1), F32)],
        compiler_params=_cparams(("arbitrary",)),
        name="outproj",
    )(o_sb, o_sw, x2, mod3, g_sb, g_sw, wo_bf, g_ffn, wr, tri)


def _tile_rows(row):
    start = row * ROW_TILE
    return pl.ds(start if isinstance(row, int) else pl.multiple_of(start, ROW_TILE), ROW_TILE)


def _tile_copy(src_ref, src_row, dst_ref, dst_row, sem):
    return pltpu.make_async_copy(src_ref.at[_tile_rows(src_row), :], dst_ref.at[_tile_rows(dst_row), :], sem)


def _for_rows_with_index(n_rows, read_index, act, group=8):
    def body(g, c):
        rows = [g * group + u for u in range(group)]
        idx = [read_index(r) for r in rows]
        for u, (r, ix) in enumerate(zip(rows, idx)):
            act(r, ix, u % 2)
        return c
    lax.fori_loop(0, n_rows // group, body, 0)


def _wait_tiles(like_ref, hbm_ref, sem):
    n = like_ref.shape[0]
    pltpu.make_async_copy(hbm_ref.at[pl.ds(0, n), :], like_ref, sem).wait()


def _dispatch_kernel(dest_ref, zblk_ref, x1_ref, mod_ref, gffn_ref, wts_ref, out_ref,
                     rows, zero_blk, sems, zero_sem):
    i = pl.program_id(0)
    n = pl.num_programs(0)
    tm = x1_ref.shape[0]
    slot = i % 2
    buf = rows.at[slot]

    @pl.when(i == 0)
    def _():
        zero_blk[...] = jnp.zeros_like(zero_blk)
        blk_rows = zero_blk.shape[0]

        def each_listed_block(act):
            def step(e, c):
                z = zblk_ref[e]

                @pl.when(z >= 0)
                def _():
                    dst = out_ref.at[pl.ds(pl.multiple_of(z * blk_rows, blk_rows), blk_rows), :]
                    act(pltpu.make_async_copy(zero_blk, dst, zero_sem))
                return c
            lax.fori_loop(0, zblk_ref.shape[0], step, 0)

        each_listed_block(lambda cp: cp.start())
        each_listed_block(lambda cp: cp.wait())

    @pl.when(i >= 2)
    def _():
        _wait_tiles(buf, out_ref, sems.at[slot])

    h2 = _ffn_input(x1_ref[...], mod_ref, gffn_ref)
    _write_row_tiles(buf, h2, jnp.concatenate([wts_ref[...], jnp.zeros((LANES - 8, tm), F32)], axis=0))

    def issue(r, dst_row, queue):
        _tile_copy(buf, r, out_ref, dst_row, sems.at[slot]).start(priority=queue)

    _for_rows_with_index(tm, lambda r: dest_ref[i * tm + r], issue)

    @pl.when(i == n - 1)
    def _():
        _wait_tiles(buf, out_ref, sems.at[slot])

        @pl.when(n >= 2)
        def _():
            _wait_tiles(rows.at[1 - slot], out_ref, sems.at[1 - slot])


def _dispatch(dest, zero_blocks, x1, mod3, g_ffn, wts, p_rows, tiles_per_batch):
    t, d = x1.shape
    tm = TM_ROUTE
    return pl.pallas_call(
        _dispatch_kernel,
        grid_spec=pltpu.PrefetchScalarGridSpec(
            num_scalar_prefetch=2,
            grid=(t // tm,),
            in_specs=[pl.BlockSpec((tm, d), lambda i, *_: (i, 0)),
                      pl.BlockSpec((1, 6, d), lambda i, *_: (i // tiles_per_batch, 0, 0)),
                      pl.BlockSpec((1, d), lambda i, *_: (0, 0)),
                      pl.BlockSpec((8, tm), lambda i, *_: (0, i))],
            out_specs=pl.BlockSpec(memory_space=pl.ANY),
            scratch_shapes=[pltpu.VMEM((2, tm * ROW_TILE, LANES), jnp.uint32),
                            pltpu.VMEM((ROW_BLOCK * ROW_TILE, LANES), jnp.uint32),
                            pltpu.SemaphoreType.DMA((2,)), pltpu.SemaphoreType.DMA(())]),
        out_shape=jax.ShapeDtypeStruct((p_rows * ROW_TILE, LANES), jnp.uint32),
        compiler_params=_cparams(("arbitrary",)),
        name="dispatch",
    )(dest, zero_blocks, x1, mod3, g_ffn, wts)


def _experts_kernel(d, ea_ref, eb_ref, nvalid_ref, x_ref, wga_ref, wua_ref, wda_ref, wgb_ref, wub_ref, wdb_ref, y_ref):
    j = pl.program_id(0)
    rb = x_ref.shape[0] // ROW_TILE
    n_word_rows = d // 2 // LANES

    @pl.when(j < nvalid_ref[0])
    def _():
        high, low = [], []
        for s in range(n_word_rows):
            w = x_ref[pl.ds(s, rb, stride=ROW_TILE), :]
            high.append(pltpu.bitcast(w & jnp.uint32(0xFFFF0000), F32).astype(BF16))
            low.append(pltpu.bitcast(w << 16, F32).astype(BF16))
        xb = jnp.concatenate(high + low, axis=1)
        wts = pltpu.bitcast(x_ref[pl.ds(n_word_rows, rb, stride=ROW_TILE), :], F32)

        def hidden(wg_ref, wu_ref, wrow):
            gt = jnp.dot(xb, wg_ref[0], preferred_element_type=F32)
            up = jnp.dot(xb, wu_ref[0], preferred_element_type=F32)
            return ((gt * jax.nn.sigmoid(gt)) * up * wrow).astype(BF16)

        ha = hidden(wga_ref, wua_ref, wts[:, 0:1])
        hb = hidden(wgb_ref, wub_ref, wts[:, 1:2])
        y = (jnp.dot(ha, wda_ref[0], preferred_element_type=F32)
             + jnp.dot(hb, wdb_ref[0], preferred_element_type=F32))
        for c in range(ROW_TILE):
            y_ref[pl.ds(c, rb, stride=ROW_TILE), :] = y[:, c * LANES:(c + 1) * LANES]

    @pl.when(j >= nvalid_ref[0])
    def _():
        y_ref[...] = jnp.zeros_like(y_ref)


def _experts(blk_ea, blk_eb, nvalid, sorted_rows, wg_bf, wu_bf, wd_bf):
    n_e, d, d_e = wg_bf.shape
    assert d == ROW_TILE * LANES, "one expert output row must fill one (8, 128) f32 tile"
    rb = ROW_BLOCK
    nb = sorted_rows.shape[0] // (rb * ROW_TILE)
    up_a = pl.BlockSpec((1, d, d_e), lambda j, ea, eb, nv: (ea[j], 0, 0))
    up_b = pl.BlockSpec((1, d, d_e), lambda j, ea, eb, nv: (eb[j], 0, 0))
    dn_a = pl.BlockSpec((1, d_e, d), lambda j, ea, eb, nv: (ea[j], 0, 0))
    dn_b = pl.BlockSpec((1, d_e, d), lambda j, ea, eb, nv: (eb[j], 0, 0))
    rows_in = rows_out = pl.BlockSpec((rb * ROW_TILE, LANES), lambda j, ea, eb, nv: (j, 0))
    return pl.pallas_call(
        functools.partial(_experts_kernel, d),
        grid_spec=pltpu.PrefetchScalarGridSpec(
            num_scalar_prefetch=3,
            grid=(nb,),
            in_specs=[rows_in, up_a, up_a, dn_a, up_b, up_b, dn_b],
            out_specs=rows_out),
        out_shape=jax.ShapeDtypeStruct(sorted_rows.shape, F32),
        compiler_params=_cparams(("arbitrary",)),
        name="experts",
    )(blk_ea, blk_eb, nvalid, sorted_rows, wg_bf, wu_bf, wd_bf, wg_bf, wu_bf, wd_bf)


def _combine_kernel(dest_ref, x1_ref, mod_ref, g_ref, y_hbm, o_ref, ybuf, sems):
    i = pl.program_id(0)
    n = pl.num_programs(0)
    tm = x1_ref.shape[0]
    slot = i % 2

    def issue(tile, slot_):
        def one(r, src_row, queue):
            _tile_copy(y_hbm, src_row, ybuf.at[slot_], r, sems.at[slot_]).start(priority=queue)
        _for_rows_with_index(tm, lambda r: dest_ref[tile * tm + r], one)

    @pl.when(i == 0)
    def _():
        issue(0, 0)

    @pl.when(i + 1 < n)
    def _():
        issue(i + 1, 1 - slot)

    _wait_tiles(ybuf.at[slot], y_hbm, sems.at[slot])
    y = jnp.concatenate([ybuf[slot, pl.ds(c, tm, stride=ROW_TILE), :] for c in range(ROW_TILE)], axis=1)
    gate2 = mod_ref[0, 5:6, :]
    o_ref[...] = _rms(x1_ref[...] + gate2 * y) * g_ref[...]


def _combine(dest, x1, mod3, g_final, y_sorted, tiles_per_batch):
    t, d = x1.shape
    tm = TM_COMB
    return pl.pallas_call(
        _combine_kernel,
        grid_spec=pltpu.PrefetchScalarGridSpec(
            num_scalar_prefetch=1,
            grid=(t // tm,),
            in_specs=[pl.BlockSpec((tm, d), lambda i, dest: (i, 0)),
                      pl.BlockSpec((1, 6, d), lambda i, dest: (i // tiles_per_batch, 0, 0)),
                      pl.BlockSpec((1, d), lambda i, dest: (0, 0)),
                      pl.BlockSpec(memory_space=pl.ANY)],
            out_specs=pl.BlockSpec((tm, d), lambda i, dest: (i, 0)),
            scratch_shapes=[pltpu.VMEM((2, tm * ROW_TILE, LANES), F32), pltpu.SemaphoreType.DMA((2,))]),
        out_shape=jax.ShapeDtypeStruct((t, d), F32),
        compiler_params=_cparams(("arbitrary",)),
        name="combine",
    )(dest, x1, mod3, g_final, y_sorted)


def _pair_table():
    ea = [g * EXPERTS_PER_GROUP + a for g in range(N_GROUPS) for a, _ in PAIR_ROLES]
    eb = [g * EXPERTS_PER_GROUP + b for g in range(N_GROUPS) for _, b in PAIR_ROLES]
    return jnp.array(ea, jnp.int32), jnp.array(eb, jnp.int32)


def _layer(x, mod3, positions, norm_mix_g, w_in, sinks, out_norm_sb_g, out_norm_sw_g, w_out, norm_ffn_g,
           w_router_group, w_router_expert, w_gate, w_up, w_down, norm_final_g):
    b, s, d = x.shape
    t = b * s
    d_sb = N_SB_HEADS * HEAD_DIM
    d_sw = N_SW_HEADS * HEAD_DIM
    d_kv = N_SW_KV_HEADS * HEAD_DIM
    scale = HEAD_DIM ** -0.5
    x2 = x.reshape(t, d)

    n_tiles = d_sw // LANES
    sw_heads = jnp.array([h for j in range(n_tiles) for h in (j, j + n_tiles)], jnp.int32)
    sw_cols = (sw_heads[:, None] * HEAD_DIM + jnp.arange(HEAD_DIM, dtype=jnp.int32)[None, :]).reshape(-1)
    q_sw_cols = 3 * d_sb + sw_cols
    w_bf = jnp.concatenate([w_in[:, :d_sb] * (scale * LOG2_E), w_in[:, d_sb:3 * d_sb],
                            w_in[:, q_sw_cols] * scale, w_in[:, 3 * d_sb + d_sw:]], axis=1).astype(BF16)

    inv_freq = ROPE_THETA ** (-jnp.arange(0, HEAD_DIM, 2, dtype=F32) / HEAD_DIM)
    freq = jnp.tile(inv_freq, LANES // (HEAD_DIM // 2)).reshape(1, LANES)
    sign = jnp.tile(jnp.concatenate([-jnp.ones(HEAD_DIM // 2, F32), jnp.ones(HEAD_DIM // 2, F32)]),
                    LANES // HEAD_DIM).reshape(1, LANES)
    n_freq = HEAD_DIM // 2
    per_row = LANES // n_freq
    pos_dense = jnp.repeat(positions.reshape(t // per_row, per_row), n_freq, axis=1)
    cos_d, sin_d = _rope_tables(pos_dense, freq)
    cos_t = jnp.tile(cos_d.reshape(t, n_freq), (1, per_row))
    sin_t = jnp.tile(sin_d.reshape(t, n_freq), (1, per_row)) * sign

    qkv_sb, q_sw, kv_sw = _inproj(x2, mod3, norm_mix_g.reshape(1, d), w_bf, cos_t, sin_t, s // TM_PROJ,
                                  3 * d_sb, d_sw)

    tri_sb = (jnp.arange(SB_BLOCK)[:, None] > jnp.arange(SB_BLOCK)[None, :]).astype(BF16)
    o_sb = _sb_attention(qkv_sb.reshape(b, s, 3 * d_sb), tri_sb).reshape(t, d_sb)
    o_sw = _sw_attention(sinks, q_sw.reshape(b, s, d_sw), kv_sw.reshape(b, s, 2 * d_kv)).reshape(t, d_sw)

    wo_bf = jnp.concatenate([w_out[:d_sb], w_out[d_sb + sw_cols]], axis=0).astype(BF16)
    wr = jnp.concatenate([w_router_group.T, w_router_expert.T,
                          jnp.zeros((META_ROWS - N_GROUPS - N_GROUPS * EXPERTS_PER_GROUP, d), F32)], axis=0)
    wr_hi = wr.astype(BF16)
    wr = jnp.concatenate([wr_hi, (wr - wr_hi.astype(F32)).astype(BF16)], axis=0)
    tri_rt = (jnp.arange(TM_ROUTE)[:, None] <= jnp.arange(TM_ROUTE)[None, :]).astype(BF16)
    g_ffn = norm_ffn_g.reshape(1, d)
    x1, meta, wts, cnt = _outproj(o_sb, o_sw, x2, mod3, out_norm_sb_g.reshape(1, d_sb),
                                  out_norm_sw_g[sw_cols].reshape(1, d_sw), wo_bf, g_ffn, wr, tri_rt, s // TM_ROUTE)

    counts = cnt[:N_BUCKETS, 0].astype(jnp.int32)
    padded = (counts + ROW_BLOCK - 1) // ROW_BLOCK * ROW_BLOCK
    pend = jnp.cumsum(padded)
    pstart = pend - padded
    dest = pstart[meta[0]] + meta[1]
    p_rows = t + N_BUCKETS * ROW_BLOCK
    nb = p_rows // ROW_BLOCK
    blk_start = jnp.arange(nb, dtype=jnp.int32) * ROW_BLOCK
    blk_bucket = jnp.minimum(jnp.sum((pend[None, :] <= blk_start[:, None]).astype(jnp.int32), axis=1), N_BUCKETS - 1)
    e_lo, e_hi = _pair_table()
    nvalid = (pend[-1:] // ROW_BLOCK).astype(jnp.int32)

    last_blk = jnp.where(padded > 0, pend // ROW_BLOCK - 1, -1)
    spare_blk = nvalid + jnp.arange(N_BUCKETS, dtype=jnp.int32)
    zero_blocks = jnp.concatenate([last_blk, jnp.where(spare_blk < nb, spare_blk, -1)]).astype(jnp.int32)
    sorted_rows = _dispatch(dest, zero_blocks, x1, mod3, g_ffn, wts, p_rows, s // TM_ROUTE)
    y_sorted = _experts(e_lo[blk_bucket], e_hi[blk_bucket], nvalid, sorted_rows,
                        w_gate.astype(BF16), w_up.astype(BF16), w_down.astype(BF16))
    return _combine(dest, x1, mod3, norm_final_g.reshape(1, d), y_sorted, s // TM_COMB).reshape(b, s, d)


def kernel(x, c, positions, w_ada, b_ada, norm_mix_g, w_in, sinks, out_norm_sb_g, out_norm_sw_g, w_out, norm_ffn_g,
           w_router_group, w_router_expert, w_gate, w_up, w_down, norm_final_g):
    b, s, d = x.shape
    depth = w_ada.shape[0]
    assert depth == 1, "the final norm is fused into the layer's last stage"
    c_pad = jnp.pad(c, ((0, 8 - b), (0, 0)))
    mod = _adaln(c_pad, w_ada[0], b_ada[0].reshape(1, -1))
    mod3 = mod[:b].reshape(b, 6, d)
    return _layer(x, mod3, positions, norm_mix_g[0], w_in[0], sinks[0], out_norm_sb_g[0], out_norm_sw_g[0],
                  w_out[0], norm_ffn_g[0], w_router_group[0], w_router_expert[0], w_gate[0], w_up[0], w_down[0],
                  norm_final_g)
```

```python
import functools

import jax
import jax.numpy as jnp
from jax import lax
from jax.experimental import pallas as pl
from jax.experimental.pallas import tpu as pltpu

F32 = jnp.float32
BF16 = jnp.bfloat16

HEAD_DIM = 64
N_SB_HEADS = 8
N_SW_HEADS = 8
N_SW_KV_HEADS = 2
WINDOW = 128
ROPE_THETA = 10000.0
N_GROUPS = 4
EXPERTS_PER_GROUP = 4
PAIR_ROLES = ((0, 1), (0, 2), (0, 3), (1, 3), (1, 2), (3, 2))
N_PAIRS = len(PAIR_ROLES)
N_BUCKETS = N_GROUPS * N_PAIRS
EPS = 1e-6

LANES = 128
VMEM_LIMIT = 56 * 1024 * 1024
SB_SKIP_LOG = -104.0

TM_PROJ = 1024
SB_BLOCK = 256
ROW_BLOCK = 256
TM_ROUTE = 512
TM_COMB = 512
SB_DENSE_LEFT = 2
SB_BLOCKS_PER_STEP = 4
SW_BLOCKS_PER_STEP = 4
META_ROWS = 32


def _cparams(sem, vmem=VMEM_LIMIT):
    return pltpu.CompilerParams(dimension_semantics=sem, vmem_limit_bytes=vmem)


def _adaln_kernel(c_ref, w_ref, b_ref, o_ref):
    c = c_ref[...]
    s = c * jax.nn.sigmoid(c)
    o_ref[...] = jnp.dot(s, w_ref[...], precision=lax.Precision.HIGHEST,
                         preferred_element_type=F32) + b_ref[...]


def _adaln(c_pad, w, b):
    rows, d = c_pad.shape
    n = w.shape[1]
    tn = 1024
    return pl.pallas_call(
        _adaln_kernel,
        grid=(n // tn,),
        in_specs=[pl.BlockSpec((rows, d), lambda j: (0, 0)),
                  pl.BlockSpec((d, tn), lambda j: (0, j)),
                  pl.BlockSpec((1, tn), lambda j: (0, j))],
        out_specs=pl.BlockSpec((rows, tn), lambda j: (0, j)),
        out_shape=jax.ShapeDtypeStruct((rows, n), F32),
        compiler_params=_cparams(("parallel",)),
        name="adaln",
    )(c_pad, w, b)


def _rope_kernel(n_freq, pos_ref, freq_ref, sign_ref, cos_ref, sin_ref):
    per_row = LANES // n_freq
    rows = pos_ref.shape[0]
    ang = pos_ref[...].astype(F32) * freq_ref[...]
    dense = (jnp.cos(ang), jnp.sin(ang))
    group = lax.broadcasted_iota(jnp.int32, (rows, LANES), 1) // n_freq
    for table, out_ref, factor in zip(dense, (cos_ref, sin_ref), (None, sign_ref[...])):
        rolled = [table] + [pltpu.roll(table, n_freq * j, 1) for j in range(1, per_row)]
        for k in range(per_row):
            full = rolled[(0 - k) % per_row]
            for g in range(1, per_row):
                full = jnp.where(group == g, rolled[(g - k) % per_row], full)
            out_ref[pl.ds(k, rows, stride=per_row), :] = full if factor is None else full * factor


def _rope_tables(pos_dense, freq, sign):
    n_freq = HEAD_DIM // 2
    per_row = LANES // n_freq
    rows = pos_dense.shape[0]
    tr = min(256, rows)
    dense = pl.BlockSpec((tr, LANES), lambda i: (i, 0))
    vec = pl.BlockSpec((1, LANES), lambda i: (0, 0))
    out = pl.BlockSpec((tr * per_row, LANES), lambda i: (i, 0))
    return pl.pallas_call(
        functools.partial(_rope_kernel, n_freq),
        grid=(rows // tr,),
        in_specs=[dense, vec, vec],
        out_specs=[out, out],
        out_shape=[jax.ShapeDtypeStruct((rows * per_row, LANES), F32)] * 2,
        compiler_params=_cparams(("parallel",)),
        name="rope",
    )(pos_dense, freq, sign)


def _rms(x):
    return x * lax.rsqrt(jnp.mean(x * x, axis=-1, keepdims=True) + EPS)


def _rotate_half_pairs(x):
    lane = lax.broadcasted_iota(jnp.int32, x.shape, 1)
    first_half = (lane % HEAD_DIM) < (HEAD_DIM // 2)
    return jnp.where(first_half, pltpu.roll(x, LANES - HEAD_DIM // 2, 1), pltpu.roll(x, HEAD_DIM // 2, 1))


def _inproj_kernel(d_sb3, d_swq, x_ref, mod_ref, g_ref, w_ref, cos_ref, sin_ref, sb_ref, q_ref, kv_ref):
    x = x_ref[...]
    shift = mod_ref[0, 0:1, :]
    scale = mod_ref[0, 1:2, :]
    h = (_rms(x) * g_ref[...]) * (1.0 + scale) + shift
    hb = h.astype(BF16)
    chunk = 512
    for c0 in range(0, d_sb3, chunk):
        sb_ref[:, c0:c0 + chunk] = jnp.dot(hb, w_ref[:, c0:c0 + chunk],
                                           preferred_element_type=F32).astype(BF16)
    cos = cos_ref[...]
    sin = sin_ref[...]
    q = jnp.dot(hb, w_ref[:, d_sb3:d_sb3 + d_swq], preferred_element_type=F32)
    for c0 in range(0, d_swq, LANES):
        qt = q[:, c0:c0 + LANES]
        q_ref[:, c0:c0 + LANES] = (qt * cos + _rotate_half_pairs(qt) * sin).astype(BF16)
    kv = jnp.dot(hb, w_ref[:, d_sb3 + d_swq:], preferred_element_type=F32)
    k = kv[:, :LANES]
    kv_ref[:, :LANES] = (k * cos + _rotate_half_pairs(k) * sin).astype(BF16)
    kv_ref[:, LANES:] = kv[:, LANES:].astype(BF16)


def _inproj(x2, mod3, g, w_bf, cos_t, sin_t, tiles_per_batch, d_sb3, d_swq):
    t, d = x2.shape
    d_in = w_bf.shape[1]
    d_kv = d_in - d_sb3 - d_swq
    tm = TM_PROJ
    return pl.pallas_call(
        functools.partial(_inproj_kernel, d_sb3, d_swq),
        grid=(t // tm,),
        in_specs=[pl.BlockSpec((tm, d), lambda i: (i, 0)),
                  pl.BlockSpec((1, 6, d), lambda i: (i // tiles_per_batch, 0, 0)),
                  pl.BlockSpec((1, d), lambda i: (0, 0)),
                  pl.BlockSpec((d, d_in), lambda i: (0, 0)),
                  pl.BlockSpec((tm, LANES), lambda i: (i, 0)),
                  pl.BlockSpec((tm, LANES), lambda i: (i, 0))],
        out_specs=[pl.BlockSpec((tm, d_sb3), lambda i: (i, 0)),
                   pl.BlockSpec((tm, d_swq), lambda i: (i, 0)),
                   pl.BlockSpec((tm, d_kv), lambda i: (i, 0))],
        out_shape=[jax.ShapeDtypeStruct((t, d_sb3), BF16),
                   jax.ShapeDtypeStruct((t, d_swq), BF16),
                   jax.ShapeDtypeStruct((t, d_kv), BF16)],
        compiler_params=_cparams(("parallel",)),
        name="inproj",
    )(x2, mod3, g, w_bf, cos_t, sin_t)


def _dot_nt(a, b):
    return lax.dot_general(a, b, (((1,), (1,)), ((), ())), preferred_element_type=F32)


def _sb_block(qm, k, v, tri, carry, causal):
    z = _dot_nt(qm, k)
    log_beta = jnp.minimum(z, 0.0) - jnp.log(1.0 + jnp.exp(-jnp.abs(z)))
    log_rest = log_beta - z
    if causal is not None:
        log_rest = jnp.where(causal, log_rest, 0.0)
    later = jnp.dot(log_rest.astype(BF16), tri, preferred_element_type=F32)
    w = jnp.exp(log_beta + later + carry)
    if causal is not None:
        w = jnp.where(causal, w, 0.0)
    out = jnp.dot(w.astype(BF16), v, preferred_element_type=F32)
    return out, carry + jnp.sum(log_rest, axis=1, keepdims=True)


def _sb_kernel(blk, n_q, q_ref, k_ref, v_ref, tri_ref, o_ref):
    i = pl.program_id(2)
    tri = tri_ref[...]
    lane = lax.broadcasted_iota(jnp.int32, (blk, LANES), 1)
    row = lax.broadcasted_iota(jnp.int32, (2 * blk, blk), 0)
    col = lax.broadcasted_iota(jnp.int32, (2 * blk, blk), 1)
    causal = col < (row % blk)

    def sweep(qs, kb, carry, causal_mask):
        s = pl.multiple_of(kb * blk, blk)
        return _sb_block(qs, k_ref[0, pl.ds(s, blk), :], v_ref[0, pl.ds(s, blk), :], tri, carry, causal_mask)

    dense = []
    for j in range(n_q):
        g = i * n_q + j
        q = q_ref[0, j * blk:(j + 1) * blk, :]
        qs = jnp.concatenate([jnp.where((lane >= h * HEAD_DIM) & (lane < (h + 1) * HEAD_DIM), q, jnp.zeros_like(q))
                              for h in range(2)], axis=0)
        acc, carry = sweep(qs, g, jnp.zeros((2 * blk, 1), F32), causal)
        for left in range(1, SB_DENSE_LEFT + 1):
            out, carry = sweep(qs, jnp.maximum(g - left, 0), jnp.where(g >= left, carry, -1e30), None)
            acc = acc + out
        dense.append((g, qs, carry, acc))

    for j, (g, qs, carry, acc) in enumerate(dense):
        def cond(state):
            kb, carry, _ = state
            return jnp.logical_and(kb >= 0, jnp.max(carry) >= SB_SKIP_LOG)

        def body(state, qs=qs):
            kb, carry, acc = state
            out, carry = sweep(qs, kb, carry, None)
            return kb - 1, carry, acc + out

        _, _, acc = lax.while_loop(cond, body, (g - 1 - SB_DENSE_LEFT, carry, acc))
        o_ref[0, j * blk:(j + 1) * blk, :] = jnp.where(lane < HEAD_DIM, acc[:blk], acc[blk:]).astype(o_ref.dtype)


def _sb_attention(qkv3, tri):
    b, s, d3 = qkv3.shape
    d_sb = d3 // 3
    n_pairs = d_sb // LANES
    blk = SB_BLOCK
    n_q = SB_BLOCKS_PER_STEP
    return pl.pallas_call(
        functools.partial(_sb_kernel, blk, n_q),
        grid=(b, n_pairs, s // (blk * n_q)),
        in_specs=[pl.BlockSpec((1, blk * n_q, LANES), lambda bi, p, i: (bi, i, p)),
                  pl.BlockSpec((1, s, LANES), lambda bi, p, i: (bi, 0, n_pairs + p)),
                  pl.BlockSpec((1, s, LANES), lambda bi, p, i: (bi, 0, 2 * n_pairs + p)),
                  pl.BlockSpec((blk, blk), lambda bi, p, i: (0, 0))],
        out_specs=pl.BlockSpec((1, blk * n_q, LANES), lambda bi, p, i: (bi, i, p)),
        out_shape=jax.ShapeDtypeStruct((b, s, d_sb), BF16),
        compiler_params=_cparams(("parallel", "parallel", "arbitrary")),
        name="sb_attn",
    )(qkv3, qkv3, qkv3, tri)


def _sw_kernel(n_tiles, n_sub, sinks_ref, q_ref, kvp_ref, kvc_ref, o_ref):
    i = pl.program_id(1)
    w = WINDOW
    qi = lax.broadcasted_iota(jnp.int32, (w, 2 * w), 0)
    kj = lax.broadcasted_iota(jnp.int32, (w, 2 * w), 1)
    diff = w + qi - kj
    in_window = (diff >= 0) & (diff < w)
    lane = lax.broadcasted_iota(jnp.int32, (w, LANES), 1)
    for j in range(n_sub):
        if j == 0:
            kk = jnp.concatenate([kvp_ref[0, :, :LANES], kvc_ref[0, :w, :LANES]], axis=0)
            vv = jnp.concatenate([kvp_ref[0, :, LANES:], kvc_ref[0, :w, LANES:]], axis=0)
            mask = in_window & ((i * n_sub) * w + kj - w >= 0)
        else:
            kk = kvc_ref[0, (j - 1) * w:(j + 1) * w, :LANES]
            vv = kvc_ref[0, (j - 1) * w:(j + 1) * w, LANES:]
            mask = in_window
        for t in range(n_tiles):
            qt = q_ref[0, j * w:(j + 1) * w, t * LANES:(t + 1) * LANES]
            outs = []
            for half in range(2):
                head = t + half * n_tiles
                in_head = (lane >= half * HEAD_DIM) & (lane < (half + 1) * HEAD_DIM)
                qm = jnp.where(in_head, qt, jnp.zeros_like(qt))
                s = jnp.where(mask, _dot_nt(qm, kk), -1e30)
                sink = sinks_ref[head]
                m = jnp.maximum(jnp.max(s, axis=-1, keepdims=True), sink)
                p = jnp.exp(s - m)
                denom = jnp.sum(p, axis=-1, keepdims=True) + jnp.exp(sink - m)
                outs.append(jnp.dot(p.astype(BF16), vv, preferred_element_type=F32) * (1.0 / denom))
            o_ref[0, j * w:(j + 1) * w, t * LANES:(t + 1) * LANES] = jnp.where(
                lane < HEAD_DIM, outs[0], outs[1]).astype(o_ref.dtype)


def _sw_attention(sinks, q3, kv3):
    b, s, dq = q3.shape
    dkv = kv3.shape[2]
    w = WINDOW
    n_sub = SW_BLOCKS_PER_STEP
    n_tiles = dq // LANES
    return pl.pallas_call(
        functools.partial(_sw_kernel, n_tiles, n_sub),
        grid=(b, s // (w * n_sub)),
        in_specs=[pl.BlockSpec(memory_space=pltpu.SMEM),
                  pl.BlockSpec((1, w * n_sub, dq), lambda bi, i: (bi, i, 0)),
                  pl.BlockSpec((1, w, dkv), lambda bi, i: (bi, jnp.maximum(i * n_sub - 1, 0), 0)),
                  pl.BlockSpec((1, w * n_sub, dkv), lambda bi, i: (bi, i, 0))],
        out_specs=pl.BlockSpec((1, w * n_sub, dq), lambda bi, i: (bi, i, 0)),
        out_shape=jax.ShapeDtypeStruct((b, s, dq), BF16),
        compiler_params=_cparams(("parallel", "parallel")),
        name="sw_attn",
    )(sinks, q3, kv3, kv3)


def _first_max(vals):
    m = vals[0]
    for v in vals[1:]:
        m = jnp.maximum(m, v)
    idx = jnp.full(m.shape, len(vals) - 1, jnp.int32)
    for k in range(len(vals) - 2, -1, -1):
        idx = jnp.where(vals[k] == m, k, idx)
    return m, idx


def _route(lg):
    g = [lg[k:k + 1, :] for k in range(N_GROUPS)]
    gmax, gsel = _first_max(g)
    gsum = g[0] * 0.0
    for k in range(N_GROUPS):
        gsum = gsum + jnp.exp(g[k] - gmax)
    gprob = 1.0 / gsum
    within = []
    for k in range(EXPERTS_PER_GROUP):
        v = lg[N_GROUPS + (N_GROUPS - 1) * EXPERTS_PER_GROUP + k:N_GROUPS + (N_GROUPS - 1) * EXPERTS_PER_GROUP + k + 1, :]
        for grp in range(N_GROUPS - 2, -1, -1):
            r = N_GROUPS + grp * EXPERTS_PER_GROUP + k
            v = jnp.where(gsel == grp, lg[r:r + 1, :], v)
        within.append(v)
    m1, i1 = _first_max(within)
    rest = [jnp.where(i1 == k, -jnp.inf, within[k]) for k in range(EXPERTS_PER_GROUP)]
    m2, i2 = _first_max(rest)
    e2 = jnp.exp(m2 - m1)
    p1 = gprob / (1.0 + e2)
    p2 = gprob * e2 / (1.0 + e2)
    pair = jnp.zeros_like(i1)
    w_a = jnp.zeros_like(p1)
    w_b = jnp.zeros_like(p1)
    for p, (a, b) in enumerate(PAIR_ROLES):
        fwd = (i1 == a) & (i2 == b)
        rev = (i1 == b) & (i2 == a)
        pair = jnp.where(fwd | rev, p, pair)
        w_a = jnp.where(fwd, p1, jnp.where(rev, p2, w_a))
        w_b = jnp.where(fwd, p2, jnp.where(rev, p1, w_b))
    return gsel * N_PAIRS + pair, w_a, w_b


ROW_TILE = 8


def _write_row_tiles(buf, h2, w_rows):
    tm, d = h2.shape
    words_per_row = d // 2
    hi = pltpu.bitcast(h2[:, :words_per_row].astype(BF16).astype(F32), jnp.uint32)
    lo = pltpu.bitcast(h2[:, words_per_row:].astype(BF16).astype(F32), jnp.uint32)
    words = hi | (lo >> 16)
    n_word_rows = words_per_row // LANES
    for s in range(n_word_rows):
        buf[pl.ds(s, tm, stride=ROW_TILE), :] = words[:, s * LANES:(s + 1) * LANES]
    buf[pl.ds(n_word_rows, tm, stride=ROW_TILE), :] = pltpu.bitcast(w_rows.T, jnp.uint32)
    for s in range(n_word_rows + 1, ROW_TILE):
        buf[pl.ds(s, tm, stride=ROW_TILE), :] = jnp.zeros((tm, LANES), jnp.uint32)


def _ffn_input(x1, mod_ref, gffn_ref):
    shift2 = mod_ref[0, 3:4, :]
    scale2 = mod_ref[0, 4:5, :]
    return (_rms(x1) * gffn_ref[...]) * (1.0 + scale2) + shift2


def _outproj_kernel(d_sb, osb_ref, osw_ref, x_ref, mod_ref, gsb_ref, gsw_ref, wo_ref, gffn_ref, wr_ref, tri_ref,
                    x1_ref, meta_ref, wts_ref, cnt_ref, run_ref):
    i = pl.program_id(0)
    tm = x_ref.shape[0]

    @pl.when(i == 0)
    def _():
        run_ref[...] = jnp.zeros_like(run_ref)

    gate1 = mod_ref[0, 2:3, :]
    a = (_rms(osb_ref[...].astype(F32)) * gsb_ref[...]).astype(BF16)
    bsw = (_rms(osw_ref[...].astype(F32)) * gsw_ref[...]).astype(BF16)
    mix = jnp.dot(a, wo_ref[:d_sb, :], preferred_element_type=F32) + jnp.dot(bsw, wo_ref[d_sb:, :],
                                                                                preferred_element_type=F32)
    x1 = x_ref[...] + gate1 * mix
    x1_ref[...] = x1
    h2 = _ffn_input(x1, mod_ref, gffn_ref)
    h2_hi = h2.astype(BF16)
    h2_lo = (h2 - h2_hi.astype(F32)).astype(BF16)
    wr2 = wr_ref[...]
    by_hi = _dot_nt(wr2, h2_hi)
    lg = by_hi[:META_ROWS] + by_hi[META_ROWS:] + _dot_nt(wr2[:META_ROWS], h2_lo)
    bucket, w_a, w_b = _route(lg)
    rows = lax.broadcasted_iota(jnp.int32, (META_ROWS, tm), 0)
    onehot = rows == bucket
    prefix = jnp.dot(jnp.where(onehot, 1.0, 0.0).astype(BF16), tri_ref[...], preferred_element_type=F32)
    run = run_ref[...]
    rank = jnp.sum(jnp.where(onehot, prefix - 1.0 + run, 0.0), axis=0, keepdims=True)
    run = run + jnp.sum(jnp.where(onehot, 1.0, 0.0), axis=1, keepdims=True)
    run_ref[...] = run
    cnt_ref[...] = jnp.broadcast_to(run, cnt_ref.shape)
    meta_ref[...] = jnp.concatenate([bucket, rank.astype(jnp.int32), jnp.zeros((6, tm), jnp.int32)], axis=0)
    wts_ref[...] = jnp.concatenate([w_a, w_b, jnp.zeros((6, tm), F32)], axis=0)


def _outproj(o_sb, o_sw, x2, mod3, g_sb, g_sw, wo_bf, g_ffn, wr, tri, tiles_per_batch):
    t, d = x2.shape
    d_sb = o_sb.shape[1]
    d_sw = o_sw.shape[1]
    tm = TM_ROUTE
    const = lambda shape: pl.BlockSpec(shape, lambda i: tuple(0 for _ in shape))
    return pl.pallas_call(
        functools.partial(_outproj_kernel, d_sb),
        grid=(t // tm,),
        in_specs=[pl.BlockSpec((tm, d_sb), lambda i: (i, 0)),
                  pl.BlockSpec((tm, d_sw), lambda i: (i, 0)),
                  pl.BlockSpec((tm, d), lambda i: (i, 0)),
                  pl.BlockSpec((1, 6, d), lambda i: (i // tiles_per_batch, 0, 0)),
                  const((1, d_sb)), const((1, d_sw)), const((d_sb + d_sw, d)), const((1, d)),
                  const((2 * META_ROWS, d)), const((tm, tm))],
        out_specs=[pl.BlockSpec((tm, d), lambda i: (i, 0)),
                   pl.BlockSpec((8, tm), lambda i: (0, i)),
                   pl.BlockSpec((8, tm), lambda i: (0, i)),
                   const((META_ROWS, LANES))],
        out_shape=[jax.ShapeDtypeStruct((t, d), F32),
                   jax.ShapeDtypeStruct((8, t), jnp.int32),
                   jax.ShapeDtypeStruct((8, t), F32),
                   jax.ShapeDtypeStruct((META_ROWS, LANES), F32)],
        scratch_shapes=[pltpu.VMEM((META_ROWS, 1), F32)],
        compiler_params=_cparams(("arbitrary",)),
        name="outproj",
    )(o_sb, o_sw, x2, mod3, g_sb, g_sw, wo_bf, g_ffn, wr, tri)


def _tile_rows(row):
    start = row * ROW_TILE
    return pl.ds(start if isinstance(row, int) else pl.multiple_of(start, ROW_TILE), ROW_TILE)


def _tile_copy(src_ref, src_row, dst_ref, dst_row, sem):
    return pltpu.make_async_copy(src_ref.at[_tile_rows(src_row), :], dst_ref.at[_tile_rows(dst_row), :], sem)


def _for_rows_with_index(n_rows, read_index, act, group=8):
    def body(g, c):
        rows = [g * group + u for u in range(group)]
        idx = [read_index(r) for r in rows]
        for u, (r, ix) in enumerate(zip(rows, idx)):
            act(r, ix, u % 2)
        return c
    lax.fori_loop(0, n_rows // group, body, 0)


def _wait_tiles(like_ref, hbm_ref, sem):
    n = like_ref.shape[0]
    pltpu.make_async_copy(hbm_ref.at[pl.ds(0, n), :], like_ref, sem).wait()


def _dispatch_kernel(dest_ref, zblk_ref, x1_ref, mod_ref, gffn_ref, wts_ref, out_ref,
                     rows, zero_blk, sems, zero_sem):
    i = pl.program_id(0)
    n = pl.num_programs(0)
    tm = x1_ref.shape[0]
    slot = i % 2
    buf = rows.at[slot]

    @pl.when(i == 0)
    def _():
        zero_blk[...] = jnp.zeros_like(zero_blk)
        blk_rows = zero_blk.shape[0]

        def each_listed_block(act):
            def step(e, c):
                z = zblk_ref[e]

                @pl.when(z >= 0)
                def _():
                    dst = out_ref.at[pl.ds(pl.multiple_of(z * blk_rows, blk_rows), blk_rows), :]
                    act(pltpu.make_async_copy(zero_blk, dst, zero_sem))
                return c
            lax.fori_loop(0, zblk_ref.shape[0], step, 0)

        each_listed_block(lambda cp: cp.start())
        each_listed_block(lambda cp: cp.wait())

    @pl.when(i >= 2)
    def _():
        _wait_tiles(buf, out_ref, sems.at[slot])

    h2 = _ffn_input(x1_ref[...], mod_ref, gffn_ref)
    _write_row_tiles(buf, h2, jnp.concatenate([wts_ref[...], jnp.zeros((LANES - 8, tm), F32)], axis=0))

    def issue(r, dst_row, queue):
        _tile_copy(buf, r, out_ref, dst_row, sems.at[slot]).start(priority=queue)

    _for_rows_with_index(tm, lambda r: dest_ref[i * tm + r], issue)

    @pl.when(i == n - 1)
    def _():
        _wait_tiles(buf, out_ref, sems.at[slot])

        @pl.when(n >= 2)
        def _():
            _wait_tiles(rows.at[1 - slot], out_ref, sems.at[1 - slot])


def _dispatch(dest, zero_blocks, x1, mod3, g_ffn, wts, p_rows, tiles_per_batch):
    t, d = x1.shape
    tm = TM_ROUTE
    return pl.pallas_call(
        _dispatch_kernel,
        grid_spec=pltpu.PrefetchScalarGridSpec(
            num_scalar_prefetch=2,
            grid=(t // tm,),
            in_specs=[pl.BlockSpec((tm, d), lambda i, *_: (i, 0)),
                      pl.BlockSpec((1, 6, d), lambda i, *_: (i // tiles_per_batch, 0, 0)),
                      pl.BlockSpec((1, d), lambda i, *_: (0, 0)),
                      pl.BlockSpec((8, tm), lambda i, *_: (0, i))],
            out_specs=pl.BlockSpec(memory_space=pl.ANY),
            scratch_shapes=[pltpu.VMEM((2, tm * ROW_TILE, LANES), jnp.uint32),
                            pltpu.VMEM((ROW_BLOCK * ROW_TILE, LANES), jnp.uint32),
                            pltpu.SemaphoreType.DMA((2,)), pltpu.SemaphoreType.DMA(())]),
        out_shape=jax.ShapeDtypeStruct((p_rows * ROW_TILE, LANES), jnp.uint32),
        compiler_params=_cparams(("arbitrary",)),
        name="dispatch",
    )(dest, zero_blocks, x1, mod3, g_ffn, wts)


def _experts_kernel(d, ea_ref, eb_ref, nvalid_ref, x_ref, wga_ref, wua_ref, wda_ref, wgb_ref, wub_ref, wdb_ref, y_ref):
    j = pl.program_id(0)
    rb = x_ref.shape[0] // ROW_TILE
    n_word_rows = d // 2 // LANES

    @pl.when(j < nvalid_ref[0])
    def _():
        high, low = [], []
        for s in range(n_word_rows):
            w = x_ref[pl.ds(s, rb, stride=ROW_TILE), :]
            high.append(pltpu.bitcast(w & jnp.uint32(0xFFFF0000), F32).astype(BF16))
            low.append(pltpu.bitcast(w << 16, F32).astype(BF16))
        xb = jnp.concatenate(high + low, axis=1)
        wts = pltpu.bitcast(x_ref[pl.ds(n_word_rows, rb, stride=ROW_TILE), :], F32)

        def hidden(wg_ref, wu_ref, wrow):
            gt = jnp.dot(xb, wg_ref[0], preferred_element_type=F32)
            up = jnp.dot(xb, wu_ref[0], preferred_element_type=F32)
            return ((gt * jax.nn.sigmoid(gt)) * up * wrow).astype(BF16)

        ha = hidden(wga_ref, wua_ref, wts[:, 0:1])
        hb = hidden(wgb_ref, wub_ref, wts[:, 1:2])
        y = (jnp.dot(ha, wda_ref[0], preferred_element_type=F32)
             + jnp.dot(hb, wdb_ref[0], preferred_element_type=F32))
        for c in range(ROW_TILE):
            y_ref[pl.ds(c, rb, stride=ROW_TILE), :] = y[:, c * LANES:(c + 1) * LANES]

    @pl.when(j >= nvalid_ref[0])
    def _():
        y_ref[...] = jnp.zeros_like(y_ref)


def _experts(blk_ea, blk_eb, nvalid, sorted_rows, wg_bf, wu_bf, wd_bf):
    n_e, d, d_e = wg_bf.shape
    assert d == ROW_TILE * LANES, "one expert output row must fill one (8, 128) f32 tile"
    rb = ROW_BLOCK
    nb = sorted_rows.shape[0] // (rb * ROW_TILE)
    up_a = pl.BlockSpec((1, d, d_e), lambda j, ea, eb, nv: (ea[j], 0, 0))
    up_b = pl.BlockSpec((1, d, d_e), lambda j, ea, eb, nv: (eb[j], 0, 0))
    dn_a = pl.BlockSpec((1, d_e, d), lambda j, ea, eb, nv: (ea[j], 0, 0))
    dn_b = pl.BlockSpec((1, d_e, d), lambda j, ea, eb, nv: (eb[j], 0, 0))
    rows_in = rows_out = pl.BlockSpec((rb * ROW_TILE, LANES), lambda j, ea, eb, nv: (j, 0))
    return pl.pallas_call(
        functools.partial(_experts_kernel, d),
        grid_spec=pltpu.PrefetchScalarGridSpec(
            num_scalar_prefetch=3,
            grid=(nb,),
            in_specs=[rows_in, up_a, up_a, dn_a, up_b, up_b, dn_b],
            out_specs=rows_out),
        out_shape=jax.ShapeDtypeStruct(sorted_rows.shape, F32),
        compiler_params=_cparams(("arbitrary",)),
        name="experts",
    )(blk_ea, blk_eb, nvalid, sorted_rows, wg_bf, wu_bf, wd_bf, wg_bf, wu_bf, wd_bf)


def _combine_kernel(dest_ref, x1_ref, mod_ref, g_ref, y_hbm, o_ref, ybuf, sems):
    i = pl.program_id(0)
    n = pl.num_programs(0)
    tm = x1_ref.shape[0]
    slot = i % 2

    def issue(tile, slot_):
        def one(r, src_row, queue):
            _tile_copy(y_hbm, src_row, ybuf.at[slot_], r, sems.at[slot_]).start(priority=queue)
        _for_rows_with_index(tm, lambda r: dest_ref[tile * tm + r], one)

    @pl.when(i == 0)
    def _():
        issue(0, 0)

    @pl.when(i + 1 < n)
    def _():
        issue(i + 1, 1 - slot)

    _wait_tiles(ybuf.at[slot], y_hbm, sems.at[slot])
    y = jnp.concatenate([ybuf[slot, pl.ds(c, tm, stride=ROW_TILE), :] for c in range(ROW_TILE)], axis=1)
    gate2 = mod_ref[0, 5:6, :]
    o_ref[...] = _rms(x1_ref[...] + gate2 * y) * g_ref[...]


def _combine(dest, x1, mod3, g_final, y_sorted, tiles_per_batch):
    t, d = x1.shape
    tm = TM_COMB
    return pl.pallas_call(
        _combine_kernel,
        grid_spec=pltpu.PrefetchScalarGridSpec(
            num_scalar_prefetch=1,
            grid=(t // tm,),
            in_specs=[pl.BlockSpec((tm, d), lambda i, dest: (i, 0)),
                      pl.BlockSpec((1, 6, d), lambda i, dest: (i // tiles_per_batch, 0, 0)),
                      pl.BlockSpec((1, d), lambda i, dest: (0, 0)),
                      pl.BlockSpec(memory_space=pl.ANY)],
            out_specs=pl.BlockSpec((tm, d), lambda i, dest: (i, 0)),
            scratch_shapes=[pltpu.VMEM((2, tm * ROW_TILE, LANES), F32), pltpu.SemaphoreType.DMA((2,))]),
        out_shape=jax.ShapeDtypeStruct((t, d), F32),
        compiler_params=_cparams(("arbitrary",)),
        name="combine",
    )(dest, x1, mod3, g_final, y_sorted)


def _pair_table():
    ea = [g * EXPERTS_PER_GROUP + a for g in range(N_GROUPS) for a, _ in PAIR_ROLES]
    eb = [g * EXPERTS_PER_GROUP + b for g in range(N_GROUPS) for _, b in PAIR_ROLES]
    return jnp.array(ea, jnp.int32), jnp.array(eb, jnp.int32)


def _layer(x, mod3, positions, norm_mix_g, w_in, sinks, out_norm_sb_g, out_norm_sw_g, w_out, norm_ffn_g,
           w_router_group, w_router_expert, w_gate, w_up, w_down, norm_final_g):
    b, s, d = x.shape
    t = b * s
    d_sb = N_SB_HEADS * HEAD_DIM
    d_sw = N_SW_HEADS * HEAD_DIM
    d_kv = N_SW_KV_HEADS * HEAD_DIM
    scale = HEAD_DIM ** -0.5
    x2 = x.reshape(t, d)

    n_tiles = d_sw // LANES
    sw_heads = jnp.array([h for j in range(n_tiles) for h in (j, j + n_tiles)], jnp.int32)
    sw_cols = (sw_heads[:, None] * HEAD_DIM + jnp.arange(HEAD_DIM, dtype=jnp.int32)[None, :]).reshape(-1)
    q_sw_cols = 3 * d_sb + sw_cols
    w_bf = jnp.concatenate([w_in[:, :d_sb] * scale, w_in[:, d_sb:3 * d_sb],
                            w_in[:, q_sw_cols] * scale, w_in[:, 3 * d_sb + d_sw:]], axis=1).astype(BF16)

    inv_freq = ROPE_THETA ** (-jnp.arange(0, HEAD_DIM, 2, dtype=F32) / HEAD_DIM)
    freq = jnp.tile(inv_freq, LANES // (HEAD_DIM // 2)).reshape(1, LANES)
    sign = jnp.tile(jnp.concatenate([-jnp.ones(HEAD_DIM // 2, F32), jnp.ones(HEAD_DIM // 2, F32)]),
                    LANES // HEAD_DIM).reshape(1, LANES)
    n_freq = HEAD_DIM // 2
    per_row = LANES // n_freq
    pos_dense = jnp.repeat(positions.reshape(t // per_row, per_row), n_freq, axis=1)
    cos_t, sin_t = _rope_tables(pos_dense, freq, sign)

    qkv_sb, q_sw, kv_sw = _inproj(x2, mod3, norm_mix_g.reshape(1, d), w_bf, cos_t, sin_t, s // TM_PROJ,
                                  3 * d_sb, d_sw)

    tri_sb = (jnp.arange(SB_BLOCK)[:, None] > jnp.arange(SB_BLOCK)[None, :]).astype(BF16)
    o_sb = _sb_attention(qkv_sb.reshape(b, s, 3 * d_sb), tri_sb).reshape(t, d_sb)
    o_sw = _sw_attention(sinks, q_sw.reshape(b, s, d_sw), kv_sw.reshape(b, s, 2 * d_kv)).reshape(t, d_sw)

    wo_bf = jnp.concatenate([w_out[:d_sb], w_out[d_sb + sw_cols]], axis=0).astype(BF16)
    wr = jnp.concatenate([w_router_group.T, w_router_expert.T,
                          jnp.zeros((META_ROWS - N_GROUPS - N_GROUPS * EXPERTS_PER_GROUP, d), F32)], axis=0)
    wr_hi = wr.astype(BF16)
    wr = jnp.concatenate([wr_hi, (wr - wr_hi.astype(F32)).astype(BF16)], axis=0)
    tri_rt = (jnp.arange(TM_ROUTE)[:, None] <= jnp.arange(TM_ROUTE)[None, :]).astype(BF16)
    g_ffn = norm_ffn_g.reshape(1, d)
    x1, meta, wts, cnt = _outproj(o_sb, o_sw, x2, mod3, out_norm_sb_g.reshape(1, d_sb),
                                  out_norm_sw_g[sw_cols].reshape(1, d_sw), wo_bf, g_ffn, wr, tri_rt, s // TM_ROUTE)

    counts = cnt[:N_BUCKETS, 0].astype(jnp.int32)
    padded = (counts + ROW_BLOCK - 1) // ROW_BLOCK * ROW_BLOCK
    pend = jnp.cumsum(padded)
    pstart = pend - padded
    bucket_ids = jnp.arange(N_BUCKETS, dtype=jnp.int32)[:, None]
    dest = jnp.sum(jnp.where(meta[0][None, :] == bucket_ids, pstart[:, None], 0), axis=0) + meta[1]
    p_rows = t + N_BUCKETS * ROW_BLOCK
    nb = p_rows // ROW_BLOCK
    blk_start = jnp.arange(nb, dtype=jnp.int32) * ROW_BLOCK
    blk_bucket = jnp.minimum(jnp.sum((pend[None, :] <= blk_start[:, None]).astype(jnp.int32), axis=1), N_BUCKETS - 1)
    e_lo, e_hi = _pair_table()
    nvalid = (pend[-1:] // ROW_BLOCK).astype(jnp.int32)

    last_blk = jnp.where(padded > 0, pend // ROW_BLOCK - 1, -1)
    spare_blk = nvalid + jnp.arange(N_BUCKETS, dtype=jnp.int32)
    zero_blocks = jnp.concatenate([last_blk, jnp.where(spare_blk < nb, spare_blk, -1)]).astype(jnp.int32)
    sorted_rows = _dispatch(dest, zero_blocks, x1, mod3, g_ffn, wts, p_rows, s // TM_ROUTE)
    y_sorted = _experts(e_lo[blk_bucket], e_hi[blk_bucket], nvalid, sorted_rows,
                        w_gate.astype(BF16), w_up.astype(BF16), w_down.astype(BF16))
    return _combine(dest, x1, mod3, norm_final_g.reshape(1, d), y_sorted, s // TM_COMB).reshape(b, s, d)


def kernel(x, c, positions, w_ada, b_ada, norm_mix_g, w_in, sinks, out_norm_sb_g, out_norm_sw_g, w_out, norm_ffn_g,
           w_router_group, w_router_expert, w_gate, w_up, w_down, norm_final_g):
    b, s, d = x.shape
    depth = w_ada.shape[0]
    assert depth == 1, "the final norm is fused into the layer's last stage"
    c_pad = jnp.pad(c, ((0, 8 - b), (0, 0)))
    mod = _adaln(c_pad, w_ada[0], b_ada[0].reshape(1, -1))
    mod3 = mod[:b].reshape(b, 6, d)
    return _layer(x, mod3, positions, norm_mix_g[0], w_in[0], sinks[0], out_norm_sb_g[0], out_norm_sw_g[0],
                  w_out[0], norm_ffn_g[0], w_router_group[0], w_router_expert[0], w_gate[0], w_up[0], w_down[0],
                  norm_final_g)
```

```python
import functools

import jax
import jax.numpy as jnp
from jax import lax
from jax.experimental import pallas as pl
from jax.experimental.pallas import tpu as pltpu

F32 = jnp.float32
BF16 = jnp.bfloat16

HEAD_DIM = 64
N_SB_HEADS = 8
N_SW_HEADS = 8
N_SW_KV_HEADS = 2
WINDOW = 128
ROPE_THETA = 10000.0
N_GROUPS = 4
EXPERTS_PER_GROUP = 4
PAIR_ROLES = ((0, 1), (0, 2), (0, 3), (1, 3), (1, 2), (3, 2))
N_PAIRS = len(PAIR_ROLES)
N_BUCKETS = N_GROUPS * N_PAIRS
EPS = 1e-6

LANES = 128
VMEM_LIMIT = 56 * 1024 * 1024
SB_SKIP_LOG = -104.0

TM_PROJ = 1024
SB_BLOCK = 256
ROW_BLOCK = 256
TM_ROUTE = 512
TM_COMB = 512
SB_DENSE_LEFT = 2
SB_BLOCKS_PER_STEP = 8
META_ROWS = 32


def _cparams(sem, vmem=VMEM_LIMIT):
    return pltpu.CompilerParams(dimension_semantics=sem, vmem_limit_bytes=vmem)


def _adaln_kernel(c_ref, w_ref, b_ref, o_ref):
    c = c_ref[...]
    s = c * jax.nn.sigmoid(c)
    o_ref[...] = jnp.dot(s, w_ref[...], precision=lax.Precision.HIGHEST,
                         preferred_element_type=F32) + b_ref[...]


def _adaln(c_pad, w, b):
    rows, d = c_pad.shape
    n = w.shape[1]
    tn = 1024
    return pl.pallas_call(
        _adaln_kernel,
        grid=(n // tn,),
        in_specs=[pl.BlockSpec((rows, d), lambda j: (0, 0)),
                  pl.BlockSpec((d, tn), lambda j: (0, j)),
                  pl.BlockSpec((1, tn), lambda j: (0, j))],
        out_specs=pl.BlockSpec((rows, tn), lambda j: (0, j)),
        out_shape=jax.ShapeDtypeStruct((rows, n), F32),
        compiler_params=_cparams(("parallel",)),
        name="adaln",
    )(c_pad, w, b)


def _rope_tables_into(cos_ref, sin_ref, n_freq, pos_ref, freq_ref, sign_ref):
    per_row = LANES // n_freq
    rows = pos_ref.shape[0]
    ang = pos_ref[...].astype(F32) * freq_ref[...]
    dense = (jnp.cos(ang), jnp.sin(ang))
    group = lax.broadcasted_iota(jnp.int32, (rows, LANES), 1) // n_freq
    for table, out_ref, factor in zip(dense, (cos_ref, sin_ref), (None, sign_ref[...])):
        rolled = [table] + [pltpu.roll(table, n_freq * j, 1) for j in range(1, per_row)]
        for k in range(per_row):
            full = rolled[(0 - k) % per_row]
            for g in range(1, per_row):
                full = jnp.where(group == g, rolled[(g - k) % per_row], full)
            out_ref[pl.ds(k, rows, stride=per_row), :] = full if factor is None else full * factor


def _rms(x):
    return x * lax.rsqrt(jnp.mean(x * x, axis=-1, keepdims=True) + EPS)


def _rotate_half_pairs(x):
    lane = lax.broadcasted_iota(jnp.int32, x.shape, 1)
    first_half = (lane % HEAD_DIM) < (HEAD_DIM // 2)
    return jnp.where(first_half, pltpu.roll(x, LANES - HEAD_DIM // 2, 1), pltpu.roll(x, HEAD_DIM // 2, 1))


def _sw_block(q_tiles, kk, vv, mask, sinks_ref):
    n_tiles = len(q_tiles)
    lane = lax.broadcasted_iota(jnp.int32, q_tiles[0].shape, 1)
    outs = []
    for t, qt in enumerate(q_tiles):
        halves = []
        for half in range(2):
            in_head = (lane >= half * HEAD_DIM) & (lane < (half + 1) * HEAD_DIM)
            qm = jnp.where(in_head, qt, jnp.zeros_like(qt))
            s = jnp.where(mask, _dot_nt(qm, kk), -1e30)
            sink = sinks_ref[t + half * n_tiles]
            m = jnp.maximum(jnp.max(s, axis=-1, keepdims=True), sink)
            p = jnp.exp(s - m)
            denom = jnp.sum(p, axis=-1, keepdims=True) + jnp.exp(sink - m)
            halves.append(jnp.dot(p.astype(BF16), vv, preferred_element_type=F32) * (1.0 / denom))
        outs.append(jnp.where(lane < HEAD_DIM, halves[0], halves[1]))
    return outs


def _inproj_kernel(d_sb3, d_swq, tiles_per_batch, sinks_ref, x_ref, mod_ref, g_ref, w_ref, pos_ref, freq_ref, sign_ref,
                   sb_ref, osw_ref, qbuf, kbuf, vbuf, cos_s, sin_s):
    i = pl.program_id(0)
    w = WINDOW
    tm = x_ref.shape[0]
    cur = i % 2
    prv = 1 - cur

    @pl.when(i == 0)
    def _():
        qbuf[...] = jnp.zeros_like(qbuf)
        kbuf[...] = jnp.zeros_like(kbuf)
        vbuf[...] = jnp.zeros_like(vbuf)

    prev_is_first = (i - 1) % tiles_per_batch == 0
    qi = lax.broadcasted_iota(jnp.int32, (w, 2 * w), 0)
    kj = lax.broadcasted_iota(jnp.int32, (w, 2 * w), 1)
    diff = w + qi - kj
    in_window = (diff >= 0) & (diff < w)
    for j in range(tm // w):
        mask = in_window & jnp.logical_or(jnp.logical_not(prev_is_first), kj >= w) if j == 0 else in_window
        q_tiles = [qbuf[prv, j * w:(j + 1) * w, c0:c0 + LANES] for c0 in range(0, d_swq, LANES)]
        outs = _sw_block(q_tiles, kbuf[prv, j * w:(j + 2) * w, :], vbuf[prv, j * w:(j + 2) * w, :], mask, sinks_ref)
        for t, o in enumerate(outs):
            osw_ref[j * w:(j + 1) * w, t * LANES:(t + 1) * LANES] = o.astype(osw_ref.dtype)

    x = x_ref[...]
    shift = mod_ref[0, 0:1, :]
    scale = mod_ref[0, 1:2, :]
    h = (_rms(x) * g_ref[...]) * (1.0 + scale) + shift
    hb = h.astype(BF16)
    chunk = 512
    for c0 in range(0, d_sb3, chunk):
        sb_ref[:, c0:c0 + chunk] = jnp.dot(hb, w_ref[:, c0:c0 + chunk],
                                           preferred_element_type=F32).astype(BF16)
    _rope_tables_into(cos_s, sin_s, HEAD_DIM // 2, pos_ref, freq_ref, sign_ref)
    cos = cos_s[...]
    sin = sin_s[...]
    q = jnp.dot(hb, w_ref[:, d_sb3:d_sb3 + d_swq], preferred_element_type=F32)
    for c0 in range(0, d_swq, LANES):
        qt = q[:, c0:c0 + LANES]
        qbuf[cur, :, c0:c0 + LANES] = (qt * cos + _rotate_half_pairs(qt) * sin).astype(BF16)
    kv = jnp.dot(hb, w_ref[:, d_sb3 + d_swq:], preferred_element_type=F32)
    k = kv[:, :LANES]
    is_first = i % tiles_per_batch == 0
    kbuf[cur, :w, :] = jnp.where(is_first, jnp.zeros((w, LANES), BF16), kbuf[prv, tm:, :])
    vbuf[cur, :w, :] = jnp.where(is_first, jnp.zeros((w, LANES), BF16), vbuf[prv, tm:, :])
    kbuf[cur, w:, :] = (k * cos + _rotate_half_pairs(k) * sin).astype(BF16)
    vbuf[cur, w:, :] = kv[:, LANES:].astype(BF16)


def _inproj(sinks, x2, mod3, g, w_bf, pos_dense, freq, sign, tiles_per_batch, d_sb3, d_swq):
    t, d = x2.shape
    d_in = w_bf.shape[1]
    tm = TM_PROJ
    n = t // tm
    this = lambda i: (jnp.minimum(i, n - 1), 0)
    prev = lambda i: (jnp.maximum(i - 1, 0), 0)
    return pl.pallas_call(
        functools.partial(_inproj_kernel, d_sb3, d_swq, tiles_per_batch),
        grid=(n + 1,),
        in_specs=[pl.BlockSpec(memory_space=pltpu.SMEM),
                  pl.BlockSpec((tm, d), this),
                  pl.BlockSpec((1, 6, d), lambda i: (jnp.minimum(i, n - 1) // tiles_per_batch, 0, 0)),
                  pl.BlockSpec((1, d), lambda i: (0, 0)),
                  pl.BlockSpec((d, d_in), lambda i: (0, 0)),
                  pl.BlockSpec((tm * pos_dense.shape[0] // t, LANES), this),
                  pl.BlockSpec((1, LANES), lambda i: (0, 0)),
                  pl.BlockSpec((1, LANES), lambda i: (0, 0))],
        out_specs=[pl.BlockSpec((tm, d_sb3), this),
                   pl.BlockSpec((tm, d_swq), prev)],
        out_shape=[jax.ShapeDtypeStruct((t, d_sb3), BF16),
                   jax.ShapeDtypeStruct((t, d_swq), BF16)],
        scratch_shapes=[pltpu.VMEM((2, tm, d_swq), BF16),
                        pltpu.VMEM((2, WINDOW + tm, LANES), BF16),
                        pltpu.VMEM((2, WINDOW + tm, LANES), BF16),
                        pltpu.VMEM((tm, LANES), F32), pltpu.VMEM((tm, LANES), F32)],
        compiler_params=_cparams(("arbitrary",)),
        name="inproj",
    )(sinks, x2, mod3, g, w_bf, pos_dense, freq, sign)


def _dot_nt(a, b):
    return lax.dot_general(a, b, (((1,), (1,)), ((), ())), preferred_element_type=F32)


def _sb_block(qm, k, v, tri, carry, causal):
    z = _dot_nt(qm, k)
    log_beta = jnp.minimum(z, 0.0) - jnp.log(1.0 + jnp.exp(-jnp.abs(z)))
    log_rest = log_beta - z
    if causal is not None:
        log_rest = jnp.where(causal, log_rest, 0.0)
    later = jnp.dot(log_rest.astype(BF16), tri, preferred_element_type=F32)
    w = jnp.exp(later + (log_beta + carry))
    if causal is not None:
        w = jnp.where(causal, w, 0.0)
    out = jnp.dot(w.astype(BF16), v, preferred_element_type=F32)
    return out, carry + jnp.sum(log_rest, axis=1, keepdims=True)


def _sb_kernel(blk, n_q, q_ref, k_ref, v_ref, tri_ref, o_ref):
    i = pl.program_id(2)
    tri = tri_ref[...]
    lane = lax.broadcasted_iota(jnp.int32, (blk, LANES), 1)
    row = lax.broadcasted_iota(jnp.int32, (2 * blk, blk), 0)
    col = lax.broadcasted_iota(jnp.int32, (2 * blk, blk), 1)
    causal = col < (row % blk)

    def sweep(qs, kb, carry, causal_mask):
        s = pl.multiple_of(kb * blk, blk)
        return _sb_block(qs, k_ref[0, pl.ds(s, blk), :], v_ref[0, pl.ds(s, blk), :], tri, carry, causal_mask)

    dense = []
    for j in range(n_q):
        g = i * n_q + j
        q = q_ref[0, j * blk:(j + 1) * blk, :]
        qs = jnp.concatenate([jnp.where((lane >= h * HEAD_DIM) & (lane < (h + 1) * HEAD_DIM), q, jnp.zeros_like(q))
                              for h in range(2)], axis=0)
        acc, carry = sweep(qs, g, jnp.zeros((2 * blk, 1), F32), causal)
        for left in range(1, SB_DENSE_LEFT + 1):
            out, carry = sweep(qs, jnp.maximum(g - left, 0), jnp.where(g >= left, carry, -1e30), None)
            acc = acc + out
        dense.append((g, qs, carry, acc))

    for j, (g, qs, carry, acc) in enumerate(dense):
        def cond(state):
            kb, carry, _ = state
            return jnp.logical_and(kb >= 0, jnp.max(carry) >= SB_SKIP_LOG)

        def body(state, qs=qs):
            kb, carry, acc = state
            out, carry = sweep(qs, kb, carry, None)
            return kb - 1, carry, acc + out

        _, _, acc = lax.while_loop(cond, body, (g - 1 - SB_DENSE_LEFT, carry, acc))
        o_ref[0, j * blk:(j + 1) * blk, :] = jnp.where(lane < HEAD_DIM, acc[:blk], acc[blk:]).astype(o_ref.dtype)


def _sb_attention(qkv3, tri):
    b, s, d3 = qkv3.shape
    d_sb = d3 // 3
    n_pairs = d_sb // LANES
    blk = SB_BLOCK
    n_q = SB_BLOCKS_PER_STEP
    return pl.pallas_call(
        functools.partial(_sb_kernel, blk, n_q),
        grid=(b, n_pairs, s // (blk * n_q)),
        in_specs=[pl.BlockSpec((1, blk * n_q, LANES), lambda bi, p, i: (bi, i, p)),
                  pl.BlockSpec((1, s, LANES), lambda bi, p, i: (bi, 0, n_pairs + p)),
                  pl.BlockSpec((1, s, LANES), lambda bi, p, i: (bi, 0, 2 * n_pairs + p)),
                  pl.BlockSpec((blk, blk), lambda bi, p, i: (0, 0))],
        out_specs=pl.BlockSpec((1, blk * n_q, LANES), lambda bi, p, i: (bi, i, p)),
        out_shape=jax.ShapeDtypeStruct((b, s, d_sb), BF16),
        compiler_params=_cparams(("parallel", "parallel", "arbitrary")),
        name="sb_attn",
    )(qkv3, qkv3, qkv3, tri)


def _first_max(vals):
    m = vals[0]
    for v in vals[1:]:
        m = jnp.maximum(m, v)
    idx = jnp.full(m.shape, len(vals) - 1, jnp.int32)
    for k in range(len(vals) - 2, -1, -1):
        idx = jnp.where(vals[k] == m, k, idx)
    return m, idx


def _route(lg):
    g = [lg[k:k + 1, :] for k in range(N_GROUPS)]
    gmax, gsel = _first_max(g)
    gsum = g[0] * 0.0
    for k in range(N_GROUPS):
        gsum = gsum + jnp.exp(g[k] - gmax)
    gprob = 1.0 / gsum
    within = []
    for k in range(EXPERTS_PER_GROUP):
        v = lg[N_GROUPS + (N_GROUPS - 1) * EXPERTS_PER_GROUP + k:N_GROUPS + (N_GROUPS - 1) * EXPERTS_PER_GROUP + k + 1, :]
        for grp in range(N_GROUPS - 2, -1, -1):
            r = N_GROUPS + grp * EXPERTS_PER_GROUP + k
            v = jnp.where(gsel == grp, lg[r:r + 1, :], v)
        within.append(v)
    m1, i1 = _first_max(within)
    rest = [jnp.where(i1 == k, -jnp.inf, within[k]) for k in range(EXPERTS_PER_GROUP)]
    m2, i2 = _first_max(rest)
    e2 = jnp.exp(m2 - m1)
    p1 = gprob / (1.0 + e2)
    p2 = gprob * e2 / (1.0 + e2)
    pair = jnp.zeros_like(i1)
    w_a = jnp.zeros_like(p1)
    w_b = jnp.zeros_like(p1)
    for p, (a, b) in enumerate(PAIR_ROLES):
        fwd = (i1 == a) & (i2 == b)
        rev = (i1 == b) & (i2 == a)
        pair = jnp.where(fwd | rev, p, pair)
        w_a = jnp.where(fwd, p1, jnp.where(rev, p2, w_a))
        w_b = jnp.where(fwd, p2, jnp.where(rev, p1, w_b))
    return gsel * N_PAIRS + pair, w_a, w_b


ROW_TILE = 8


def _write_row_tiles(buf, h2, w_rows):
    tm, d = h2.shape
    words_per_row = d // 2
    hi = pltpu.bitcast(h2[:, :words_per_row].astype(BF16).astype(F32), jnp.uint32)
    lo = pltpu.bitcast(h2[:, words_per_row:].astype(BF16).astype(F32), jnp.uint32)
    words = hi | (lo >> 16)
    n_word_rows = words_per_row // LANES
    for s in range(n_word_rows):
        buf[pl.ds(s, tm, stride=ROW_TILE), :] = words[:, s * LANES:(s + 1) * LANES]
    buf[pl.ds(n_word_rows, tm, stride=ROW_TILE), :] = pltpu.bitcast(w_rows.T, jnp.uint32)
    for s in range(n_word_rows + 1, ROW_TILE):
        buf[pl.ds(s, tm, stride=ROW_TILE), :] = jnp.zeros((tm, LANES), jnp.uint32)


def _ffn_input(x1, mod_ref, gffn_ref):
    shift2 = mod_ref[0, 3:4, :]
    scale2 = mod_ref[0, 4:5, :]
    return (_rms(x1) * gffn_ref[...]) * (1.0 + scale2) + shift2


def _outproj_kernel(d_sb, osb_ref, osw_ref, x_ref, mod_ref, gsb_ref, gsw_ref, wo_ref, gffn_ref, wr_ref, tri_ref,
                    x1_ref, meta_ref, wts_ref, cnt_ref, run_ref):
    i = pl.program_id(0)
    tm = x_ref.shape[0]

    @pl.when(i == 0)
    def _():
        run_ref[...] = jnp.zeros_like(run_ref)

    gate1 = mod_ref[0, 2:3, :]
    a = (_rms(osb_ref[...].astype(F32)) * gsb_ref[...]).astype(BF16)
    bsw = (_rms(osw_ref[...].astype(F32)) * gsw_ref[...]).astype(BF16)
    mix = jnp.dot(a, wo_ref[:d_sb, :], preferred_element_type=F32) + jnp.dot(bsw, wo_ref[d_sb:, :],
                                                                                preferred_element_type=F32)
    x1 = x_ref[...] + gate1 * mix
    x1_ref[...] = x1
    h2 = _ffn_input(x1, mod_ref, gffn_ref)
    h2_hi = h2.astype(BF16)
    h2_lo = (h2 - h2_hi.astype(F32)).astype(BF16)
    wr2 = wr_ref[...]
    by_hi = _dot_nt(wr2, h2_hi)
    lg = by_hi[:META_ROWS] + by_hi[META_ROWS:] + _dot_nt(wr2[:META_ROWS], h2_lo)
    bucket, w_a, w_b = _route(lg)
    rows = lax.broadcasted_iota(jnp.int32, (META_ROWS, tm), 0)
    onehot = rows == bucket
    prefix = jnp.dot(jnp.where(onehot, 1.0, 0.0).astype(BF16), tri_ref[...], preferred_element_type=F32)
    run = run_ref[...]
    rank = jnp.sum(jnp.where(onehot, prefix - 1.0 + run, 0.0), axis=0, keepdims=True)
    run = run + jnp.sum(jnp.where(onehot, 1.0, 0.0), axis=1, keepdims=True)
    run_ref[...] = run
    cnt_ref[...] = jnp.broadcast_to(run, cnt_ref.shape)
    meta_ref[...] = jnp.concatenate([bucket, rank.astype(jnp.int32), jnp.zeros((6, tm), jnp.int32)], axis=0)
    wts_ref[...] = jnp.concatenate([w_a, w_b, jnp.zeros((6, tm), F32)], axis=0)


def _outproj(o_sb, o_sw, x2, mod3, g_sb, g_sw, wo_bf, g_ffn, wr, tri, tiles_per_batch):
    t, d = x2.shape
    d_sb = o_sb.shape[1]
    d_sw = o_sw.shape[1]
    tm = TM_ROUTE
    const = lambda shape: pl.BlockSpec(shape, lambda i: tuple(0 for _ in shape))
    return pl.pallas_call(
        functools.partial(_outproj_kernel, d_sb),
        grid=(t // tm,),
        in_specs=[pl.BlockSpec((tm, d_sb), lambda i: (i, 0)),
                  pl.BlockSpec((tm, d_sw), lambda i: (i, 0)),
                  pl.BlockSpec((tm, d), lambda i: (i, 0)),
                  pl.BlockSpec((1, 6, d), lambda i: (i // tiles_per_batch, 0, 0)),
                  const((1, d_sb)), const((1, d_sw)), const((d_sb + d_sw, d)), const((1, d)),
                  const((2 * META_ROWS, d)), const((tm, tm))],
        out_specs=[pl.BlockSpec((tm, d), lambda i: (i, 0)),
                   pl.BlockSpec((8, tm), lambda i: (0, i)),
                   pl.BlockSpec((8, tm), lambda i: (0, i)),
                   const((META_ROWS, LANES))],
        out_shape=[jax.ShapeDtypeStruct((t, d), F32),
                   jax.ShapeDtypeStruct((8, t), jnp.int32),
                   jax.ShapeDtypeStruct((8, t), F32),
                   jax.ShapeDtypeStruct((META_ROWS, LANES), F32)],
        scratch_shapes=[pltpu.VMEM((META_ROWS, 1), F32)],
        compiler_params=_cparams(("arbitrary",)),
        name="outproj",
    )(o_sb, o_sw, x2, mod3, g_sb, g_sw, wo_bf, g_ffn, wr, tri)


def _tile_rows(row):
    start = row * ROW_TILE
    return pl.ds(start if isinstance(row, int) else pl.multiple_of(start, ROW_TILE), ROW_TILE)


def _tile_copy(src_ref, src_row, dst_ref, dst_row, sem):
    return pltpu.make_async_copy(src_ref.at[_tile_rows(src_row), :], dst_ref.at[_tile_rows(dst_row), :], sem)


def _for_rows_with_index(n_rows, read_index, act, group=8):
    def body(g, c):
        rows = [g * group + u for u in range(group)]
        idx = [read_index(r) for r in rows]
        for u, (r, ix) in enumerate(zip(rows, idx)):
            act(r, ix, u % 2)
        return c
    lax.fori_loop(0, n_rows // group, body, 0)


def _wait_tiles(like_ref, hbm_ref, sem):
    n = like_ref.shape[0]
    pltpu.make_async_copy(hbm_ref.at[pl.ds(0, n), :], like_ref, sem).wait()


def _dispatch_kernel(dest_ref, zblk_ref, x1_ref, mod_ref, gffn_ref, wts_ref, out_ref,
                     rows, zero_blk, sems, zero_sem):
    i = pl.program_id(0)
    n = pl.num_programs(0)
    tm = x1_ref.shape[0]
    slot = i % 2
    buf = rows.at[slot]

    @pl.when(i == 0)
    def _():
        zero_blk[...] = jnp.zeros_like(zero_blk)
        blk_rows = zero_blk.shape[0]

        def each_listed_block(act):
            def step(e, c):
                z = zblk_ref[e]

                @pl.when(z >= 0)
                def _():
                    dst = out_ref.at[pl.ds(pl.multiple_of(z * blk_rows, blk_rows), blk_rows), :]
                    act(pltpu.make_async_copy(zero_blk, dst, zero_sem))
                return c
            lax.fori_loop(0, zblk_ref.shape[0], step, 0)

        each_listed_block(lambda cp: cp.start())
        each_listed_block(lambda cp: cp.wait())

    @pl.when(i >= 2)
    def _():
        _wait_tiles(buf, out_ref, sems.at[slot])

    h2 = _ffn_input(x1_ref[...], mod_ref, gffn_ref)
    _write_row_tiles(buf, h2, jnp.concatenate([wts_ref[...], jnp.zeros((LANES - 8, tm), F32)], axis=0))

    def issue(r, dst_row, queue):
        _tile_copy(buf, r, out_ref, dst_row, sems.at[slot]).start(priority=queue)

    _for_rows_with_index(tm, lambda r: dest_ref[i * tm + r], issue)

    @pl.when(i == n - 1)
    def _():
        _wait_tiles(buf, out_ref, sems.at[slot])

        @pl.when(n >= 2)
        def _():
            _wait_tiles(rows.at[1 - slot], out_ref, sems.at[1 - slot])


def _dispatch(dest, zero_blocks, x1, mod3, g_ffn, wts, p_rows, tiles_per_batch):
    t, d = x1.shape
    tm = TM_ROUTE
    return pl.pallas_call(
        _dispatch_kernel,
        grid_spec=pltpu.PrefetchScalarGridSpec(
            num_scalar_prefetch=2,
            grid=(t // tm,),
            in_specs=[pl.BlockSpec((tm, d), lambda i, *_: (i, 0)),
                      pl.BlockSpec((1, 6, d), lambda i, *_: (i // tiles_per_batch, 0, 0)),
                      pl.BlockSpec((1, d), lambda i, *_: (0, 0)),
                      pl.BlockSpec((8, tm), lambda i, *_: (0, i))],
            out_specs=pl.BlockSpec(memory_space=pl.ANY),
            scratch_shapes=[pltpu.VMEM((2, tm * ROW_TILE, LANES), jnp.uint32),
                            pltpu.VMEM((ROW_BLOCK * ROW_TILE, LANES), jnp.uint32),
                            pltpu.SemaphoreType.DMA((2,)), pltpu.SemaphoreType.DMA(())]),
        out_shape=jax.ShapeDtypeStruct((p_rows * ROW_TILE, LANES), jnp.uint32),
        compiler_params=_cparams(("arbitrary",)),
        name="dispatch",
    )(dest, zero_blocks, x1, mod3, g_ffn, wts)


def _experts_kernel(d, ea_ref, eb_ref, nvalid_ref, x_ref, wga_ref, wua_ref, wda_ref, wgb_ref, wub_ref, wdb_ref, y_ref):
    j = pl.program_id(0)
    rb = x_ref.shape[0] // ROW_TILE
    n_word_rows = d // 2 // LANES

    @pl.when(j < nvalid_ref[0])
    def _():
        high, low = [], []
        for s in range(n_word_rows):
            w = x_ref[pl.ds(s, rb, stride=ROW_TILE), :]
            high.append(pltpu.bitcast(w & jnp.uint32(0xFFFF0000), F32).astype(BF16))
            low.append(pltpu.bitcast(w << 16, F32).astype(BF16))
        xb = jnp.concatenate(high + low, axis=1)
        wts = pltpu.bitcast(x_ref[pl.ds(n_word_rows, rb, stride=ROW_TILE), :], F32)

        def hidden(wg_ref, wu_ref, wrow):
            gt = jnp.dot(xb, wg_ref[0], preferred_element_type=F32)
            up = jnp.dot(xb, wu_ref[0], preferred_element_type=F32)
            return ((gt * jax.nn.sigmoid(gt)) * up * wrow).astype(BF16)

        ha = hidden(wga_ref, wua_ref, wts[:, 0:1])
        hb = hidden(wgb_ref, wub_ref, wts[:, 1:2])
        y = (jnp.dot(ha, wda_ref[0], preferred_element_type=F32)
             + jnp.dot(hb, wdb_ref[0], preferred_element_type=F32))
        for c in range(ROW_TILE):
            y_ref[pl.ds(c, rb, stride=ROW_TILE), :] = y[:, c * LANES:(c + 1) * LANES]

    @pl.when(j >= nvalid_ref[0])
    def _():
        y_ref[...] = jnp.zeros_like(y_ref)


def _experts(blk_ea, blk_eb, nvalid, sorted_rows, wg_bf, wu_bf, wd_bf):
    n_e, d, d_e = wg_bf.shape
    assert d == ROW_TILE * LANES, "one expert output row must fill one (8, 128) f32 tile"
    rb = ROW_BLOCK
    nb = sorted_rows.shape[0] // (rb * ROW_TILE)
    up_a = pl.BlockSpec((1, d, d_e), lambda j, ea, eb, nv: (ea[j], 0, 0))
    up_b = pl.BlockSpec((1, d, d_e), lambda j, ea, eb, nv: (eb[j], 0, 0))
    dn_a = pl.BlockSpec((1, d_e, d), lambda j, ea, eb, nv: (ea[j], 0, 0))
    dn_b = pl.BlockSpec((1, d_e, d), lambda j, ea, eb, nv: (eb[j], 0, 0))
    rows_in = rows_out = pl.BlockSpec((rb * ROW_TILE, LANES), lambda j, ea, eb, nv: (j, 0))
    return pl.pallas_call(
        functools.partial(_experts_kernel, d),
        grid_spec=pltpu.PrefetchScalarGridSpec(
            num_scalar_prefetch=3,
            grid=(nb,),
            in_specs=[rows_in, up_a, up_a, dn_a, up_b, up_b, dn_b],
            out_specs=rows_out),
        out_shape=jax.ShapeDtypeStruct(sorted_rows.shape, F32),
        compiler_params=_cparams(("arbitrary",)),
        name="experts",
    )(blk_ea, blk_eb, nvalid, sorted_rows, wg_bf, wu_bf, wd_bf, wg_bf, wu_bf, wd_bf)


def _combine_kernel(dest_ref, x1_ref, mod_ref, g_ref, y_hbm, o_ref, ybuf, sems):
    i = pl.program_id(0)
    n = pl.num_programs(0)
    tm = x1_ref.shape[0]
    slot = i % 2

    def issue(tile, slot_):
        def one(r, src_row, queue):
            _tile_copy(y_hbm, src_row, ybuf.at[slot_], r, sems.at[slot_]).start(priority=queue)
        _for_rows_with_index(tm, lambda r: dest_ref[tile * tm + r], one)

    @pl.when(i == 0)
    def _():
        issue(0, 0)

    @pl.when(i + 1 < n)
    def _():
        issue(i + 1, 1 - slot)

    _wait_tiles(ybuf.at[slot], y_hbm, sems.at[slot])
    y = jnp.concatenate([ybuf[slot, pl.ds(c, tm, stride=ROW_TILE), :] for c in range(ROW_TILE)], axis=1)
    gate2 = mod_ref[0, 5:6, :]
    o_ref[...] = _rms(x1_ref[...] + gate2 * y) * g_ref[...]


def _combine(dest, x1, mod3, g_final, y_sorted, tiles_per_batch):
    t, d = x1.shape
    tm = TM_COMB
    return pl.pallas_call(
        _combine_kernel,
        grid_spec=pltpu.PrefetchScalarGridSpec(
            num_scalar_prefetch=1,
            grid=(t // tm,),
            in_specs=[pl.BlockSpec((tm, d), lambda i, dest: (i, 0)),
                      pl.BlockSpec((1, 6, d), lambda i, dest: (i // tiles_per_batch, 0, 0)),
                      pl.BlockSpec((1, d), lambda i, dest: (0, 0)),
                      pl.BlockSpec(memory_space=pl.ANY)],
            out_specs=pl.BlockSpec((tm, d), lambda i, dest: (i, 0)),
            scratch_shapes=[pltpu.VMEM((2, tm * ROW_TILE, LANES), F32), pltpu.SemaphoreType.DMA((2,))]),
        out_shape=jax.ShapeDtypeStruct((t, d), F32),
        compiler_params=_cparams(("arbitrary",)),
        name="combine",
    )(dest, x1, mod3, g_final, y_sorted)


def _pair_table():
    ea = [g * EXPERTS_PER_GROUP + a for g in range(N_GROUPS) for a, _ in PAIR_ROLES]
    eb = [g * EXPERTS_PER_GROUP + b for g in range(N_GROUPS) for _, b in PAIR_ROLES]
    return jnp.array(ea, jnp.int32), jnp.array(eb, jnp.int32)


def _layer(x, mod3, positions, norm_mix_g, w_in, sinks, out_norm_sb_g, out_norm_sw_g, w_out, norm_ffn_g,
           w_router_group, w_router_expert, w_gate, w_up, w_down, norm_final_g):
    b, s, d = x.shape
    t = b * s
    d_sb = N_SB_HEADS * HEAD_DIM
    d_sw = N_SW_HEADS * HEAD_DIM
    scale = HEAD_DIM ** -0.5
    x2 = x.reshape(t, d)

    n_tiles = d_sw // LANES
    sw_heads = jnp.array([h for j in range(n_tiles) for h in (j, j + n_tiles)], jnp.int32)
    sw_cols = (sw_heads[:, None] * HEAD_DIM + jnp.arange(HEAD_DIM, dtype=jnp.int32)[None, :]).reshape(-1)
    q_sw_cols = 3 * d_sb + sw_cols
    w_bf = jnp.concatenate([w_in[:, :d_sb] * scale, w_in[:, d_sb:3 * d_sb],
                            w_in[:, q_sw_cols] * scale, w_in[:, 3 * d_sb + d_sw:]], axis=1).astype(BF16)

    inv_freq = ROPE_THETA ** (-jnp.arange(0, HEAD_DIM, 2, dtype=F32) / HEAD_DIM)
    freq = jnp.tile(inv_freq, LANES // (HEAD_DIM // 2)).reshape(1, LANES)
    sign = jnp.tile(jnp.concatenate([-jnp.ones(HEAD_DIM // 2, F32), jnp.ones(HEAD_DIM // 2, F32)]),
                    LANES // HEAD_DIM).reshape(1, LANES)
    n_freq = HEAD_DIM // 2
    per_row = LANES // n_freq
    pos_dense = jnp.repeat(positions.reshape(t // per_row, per_row), n_freq, axis=1)

    qkv_sb, o_sw = _inproj(sinks, x2, mod3, norm_mix_g.reshape(1, d), w_bf, pos_dense, freq, sign, s // TM_PROJ,
                           3 * d_sb, d_sw)

    tri_sb = (jnp.arange(SB_BLOCK)[:, None] > jnp.arange(SB_BLOCK)[None, :]).astype(BF16)
    o_sb = _sb_attention(qkv_sb.reshape(b, s, 3 * d_sb), tri_sb).reshape(t, d_sb)

    wo_bf = jnp.concatenate([w_out[:d_sb], w_out[d_sb + sw_cols]], axis=0).astype(BF16)
    wr = jnp.concatenate([w_router_group.T, w_router_expert.T,
                          jnp.zeros((META_ROWS - N_GROUPS - N_GROUPS * EXPERTS_PER_GROUP, d), F32)], axis=0)
    wr_hi = wr.astype(BF16)
    wr = jnp.concatenate([wr_hi, (wr - wr_hi.astype(F32)).astype(BF16)], axis=0)
    tri_rt = (jnp.arange(TM_ROUTE)[:, None] <= jnp.arange(TM_ROUTE)[None, :]).astype(BF16)
    g_ffn = norm_ffn_g.reshape(1, d)
    x1, meta, wts, cnt = _outproj(o_sb, o_sw, x2, mod3, out_norm_sb_g.reshape(1, d_sb),
                                  out_norm_sw_g[sw_cols].reshape(1, d_sw), wo_bf, g_ffn, wr, tri_rt, s // TM_ROUTE)

    counts = cnt[:N_BUCKETS, 0].astype(jnp.int32)
    padded = (counts + ROW_BLOCK - 1) // ROW_BLOCK * ROW_BLOCK
    pend = jnp.cumsum(padded)
    pstart = pend - padded
    bucket_ids = jnp.arange(N_BUCKETS, dtype=jnp.int32)[:, None]
    dest = jnp.sum(jnp.where(meta[0][None, :] == bucket_ids, pstart[:, None], 0), axis=0) + meta[1]
    p_rows = t + N_BUCKETS * ROW_BLOCK
    nb = p_rows // ROW_BLOCK
    blk_start = jnp.arange(nb, dtype=jnp.int32) * ROW_BLOCK
    blk_bucket = jnp.minimum(jnp.sum((pend[None, :] <= blk_start[:, None]).astype(jnp.int32), axis=1), N_BUCKETS - 1)
    e_lo, e_hi = _pair_table()
    nvalid = (pend[-1:] // ROW_BLOCK).astype(jnp.int32)

    last_blk = jnp.where(padded > 0, pend // ROW_BLOCK - 1, -1)
    spare_blk = nvalid + jnp.arange(N_BUCKETS, dtype=jnp.int32)
    zero_blocks = jnp.concatenate([last_blk, jnp.where(spare_blk < nb, spare_blk, -1)]).astype(jnp.int32)
    sorted_rows = _dispatch(dest, zero_blocks, x1, mod3, g_ffn, wts, p_rows, s // TM_ROUTE)
    y_sorted = _experts(e_lo[blk_bucket], e_hi[blk_bucket], nvalid, sorted_rows,
                        w_gate.astype(BF16), w_up.astype(BF16), w_down.astype(BF16))
    return _combine(dest, x1, mod3, norm_final_g.reshape(1, d), y_sorted, s // TM_COMB).reshape(b, s, d)


def kernel(x, c, positions, w_ada, b_ada, norm_mix_g, w_in, sinks, out_norm_sb_g, out_norm_sw_g, w_out, norm_ffn_g,
           w_router_group, w_router_expert, w_gate, w_up, w_down, norm_final_g):
    b, s, d = x.shape
    depth = w_ada.shape[0]
    assert depth == 1, "the final norm is fused into the layer's last stage"
    c_pad = jnp.pad(c, ((0, 8 - b), (0, 0)))
    mod = _adaln(c_pad, w_ada[0], b_ada[0].reshape(1, -1))
    mod3 = mod[:b].reshape(b, 6, d)
    return _layer(x, mod3, positions, norm_mix_g[0], w_in[0], sinks[0], out_norm_sb_g[0], out_norm_sw_g[0],
                  w_out[0], norm_ffn_g[0], w_router_group[0], w_router_expert[0], w_gate[0], w_up[0], w_down[0],
                  norm_final_g)
```

```python
import functools

import jax
import jax.numpy as jnp
from jax import lax
from jax.experimental import pallas as pl
from jax.experimental.pallas import tpu as pltpu

F32 = jnp.float32
BF16 = jnp.bfloat16

HEAD_DIM = 64
N_SB_HEADS = 8
N_SW_HEADS = 8
N_SW_KV_HEADS = 2
WINDOW = 128
ROPE_THETA = 10000.0
N_GROUPS = 4
EXPERTS_PER_GROUP = 4
PAIR_ROLES = ((0, 1), (0, 2), (0, 3), (1, 3), (1, 2), (3, 2))
N_PAIRS = len(PAIR_ROLES)
N_BUCKETS = N_GROUPS * N_PAIRS
EPS = 1e-6

LANES = 128
VMEM_LIMIT = 56 * 1024 * 1024
SB_SKIP_LOG = -104.0

TM_PROJ = 1024
SB_BLOCK = 256
ROW_BLOCK = 256
TM_ROUTE = 512
TM_COMB = 512
SB_DENSE_LEFT = 2
SB_BLOCKS_PER_STEP = 8
META_ROWS = 32


def _cparams(sem, vmem=VMEM_LIMIT):
    return pltpu.CompilerParams(dimension_semantics=sem, vmem_limit_bytes=vmem)


def _adaln_kernel(c_ref, w_ref, b_ref, o_ref):
    c = c_ref[...]
    s = c * jax.nn.sigmoid(c)
    o_ref[...] = jnp.dot(s, w_ref[...], precision=lax.Precision.HIGHEST,
                         preferred_element_type=F32) + b_ref[...]


def _adaln(c_pad, w, b):
    rows, d = c_pad.shape
    n = w.shape[1]
    tn = 1024
    return pl.pallas_call(
        _adaln_kernel,
        grid=(n // tn,),
        in_specs=[pl.BlockSpec((rows, d), lambda j: (0, 0)),
                  pl.BlockSpec((d, tn), lambda j: (0, j)),
                  pl.BlockSpec((1, tn), lambda j: (0, j))],
        out_specs=pl.BlockSpec((rows, tn), lambda j: (0, j)),
        out_shape=jax.ShapeDtypeStruct((rows, n), F32),
        compiler_params=_cparams(("parallel",)),
        name="adaln",
    )(c_pad, w, b)


def _rope_tables_into(cos_ref, sin_ref, n_freq, pos_ref, freq_ref, sign_ref):
    per_row = LANES // n_freq
    rows = pos_ref.shape[0]
    ang = pos_ref[...].astype(F32) * freq_ref[...]
    dense = (jnp.cos(ang), jnp.sin(ang))
    group = lax.broadcasted_iota(jnp.int32, (rows, LANES), 1) // n_freq
    for table, out_ref, factor in zip(dense, (cos_ref, sin_ref), (None, sign_ref[...])):
        rolled = [table] + [pltpu.roll(table, n_freq * j, 1) for j in range(1, per_row)]
        for k in range(per_row):
            full = rolled[(0 - k) % per_row]
            for g in range(1, per_row):
                full = jnp.where(group == g, rolled[(g - k) % per_row], full)
            out_ref[pl.ds(k, rows, stride=per_row), :] = full if factor is None else full * factor


def _rms(x):
    return x * lax.rsqrt(jnp.mean(x * x, axis=-1, keepdims=True) + EPS)


def _rotate_half_pairs(x):
    lane = lax.broadcasted_iota(jnp.int32, x.shape, 1)
    first_half = (lane % HEAD_DIM) < (HEAD_DIM // 2)
    return jnp.where(first_half, pltpu.roll(x, LANES - HEAD_DIM // 2, 1), pltpu.roll(x, HEAD_DIM // 2, 1))


def _sw_block(q_tiles, kk, vv, mask, sinks_ref):
    n_tiles = len(q_tiles)
    lane = lax.broadcasted_iota(jnp.int32, q_tiles[0].shape, 1)
    outs = []
    for t, qt in enumerate(q_tiles):
        halves = []
        for half in range(2):
            in_head = (lane >= half * HEAD_DIM) & (lane < (half + 1) * HEAD_DIM)
            qm = jnp.where(in_head, qt, jnp.zeros_like(qt))
            s = jnp.where(mask, _dot_nt(qm, kk), -1e30)
            sink = sinks_ref[t + half * n_tiles]
            m = jnp.maximum(jnp.max(s, axis=-1, keepdims=True), sink)
            p = jnp.exp(s - m)
            denom = jnp.sum(p, axis=-1, keepdims=True) + jnp.exp(sink - m)
            halves.append(jnp.dot(p.astype(BF16), vv, preferred_element_type=F32) * (1.0 / denom))
        outs.append(jnp.where(lane < HEAD_DIM, halves[0], halves[1]))
    return outs


def _inproj_kernel(d_sb3, d_swq, tiles_per_batch, sinks_ref, x_ref, mod_ref, g_ref, w_ref, pos_ref, freq_ref, sign_ref,
                   sb_ref, osw_ref, qbuf, kbuf, vbuf, cos_s, sin_s):
    i = pl.program_id(0)
    w = WINDOW
    tm = x_ref.shape[0]
    cur = i % 2
    prv = 1 - cur

    @pl.when(i == 0)
    def _():
        qbuf[...] = jnp.zeros_like(qbuf)
        kbuf[...] = jnp.zeros_like(kbuf)
        vbuf[...] = jnp.zeros_like(vbuf)

    prev_is_first = (i - 1) % tiles_per_batch == 0
    qi = lax.broadcasted_iota(jnp.int32, (w, 2 * w), 0)
    kj = lax.broadcasted_iota(jnp.int32, (w, 2 * w), 1)
    diff = w + qi - kj
    in_window = (diff >= 0) & (diff < w)
    for j in range(tm // w):
        mask = in_window & jnp.logical_or(jnp.logical_not(prev_is_first), kj >= w) if j == 0 else in_window
        q_tiles = [qbuf[prv, j * w:(j + 1) * w, c0:c0 + LANES] for c0 in range(0, d_swq, LANES)]
        outs = _sw_block(q_tiles, kbuf[prv, j * w:(j + 2) * w, :], vbuf[prv, j * w:(j + 2) * w, :], mask, sinks_ref)
        for t, o in enumerate(outs):
            osw_ref[j * w:(j + 1) * w, t * LANES:(t + 1) * LANES] = o.astype(osw_ref.dtype)

    x = x_ref[...]
    shift = mod_ref[0, 0:1, :]
    scale = mod_ref[0, 1:2, :]
    h = (_rms(x) * g_ref[...]) * (1.0 + scale) + shift
    hb = h.astype(BF16)
    chunk = 512
    for c0 in range(0, d_sb3, chunk):
        sb_ref[:, c0:c0 + chunk] = jnp.dot(hb, w_ref[:, c0:c0 + chunk],
                                           preferred_element_type=F32).astype(BF16)
    _rope_tables_into(cos_s, sin_s, HEAD_DIM // 2, pos_ref, freq_ref, sign_ref)
    cos = cos_s[...]
    sin = sin_s[...]
    q = jnp.dot(hb, w_ref[:, d_sb3:d_sb3 + d_swq], preferred_element_type=F32)
    for c0 in range(0, d_swq, LANES):
        qt = q[:, c0:c0 + LANES]
        qbuf[cur, :, c0:c0 + LANES] = (qt * cos + _rotate_half_pairs(qt) * sin).astype(BF16)
    kv = jnp.dot(hb, w_ref[:, d_sb3 + d_swq:], preferred_element_type=F32)
    k = kv[:, :LANES]
    is_first = i % tiles_per_batch == 0
    kbuf[cur, :w, :] = jnp.where(is_first, jnp.zeros((w, LANES), BF16), kbuf[prv, tm:, :])
    vbuf[cur, :w, :] = jnp.where(is_first, jnp.zeros((w, LANES), BF16), vbuf[prv, tm:, :])
    kbuf[cur, w:, :] = (k * cos + _rotate_half_pairs(k) * sin).astype(BF16)
    vbuf[cur, w:, :] = kv[:, LANES:].astype(BF16)


def _inproj(sinks, x2, mod3, g, w_bf, pos_dense, freq, sign, tiles_per_batch, d_sb3, d_swq):
    t, d = x2.shape
    d_in = w_bf.shape[1]
    tm = TM_PROJ
    n = t // tm
    this = lambda i: (jnp.minimum(i, n - 1), 0)
    prev = lambda i: (jnp.maximum(i - 1, 0), 0)
    return pl.pallas_call(
        functools.partial(_inproj_kernel, d_sb3, d_swq, tiles_per_batch),
        grid=(n + 1,),
        in_specs=[pl.BlockSpec(memory_space=pltpu.SMEM),
                  pl.BlockSpec((tm, d), this),
                  pl.BlockSpec((1, 6, d), lambda i: (jnp.minimum(i, n - 1) // tiles_per_batch, 0, 0)),
                  pl.BlockSpec((1, d), lambda i: (0, 0)),
                  pl.BlockSpec((d, d_in), lambda i: (0, 0)),
                  pl.BlockSpec((tm * pos_dense.shape[0] // t, LANES), this),
                  pl.BlockSpec((1, LANES), lambda i: (0, 0)),
                  pl.BlockSpec((1, LANES), lambda i: (0, 0))],
        out_specs=[pl.BlockSpec((tm, d_sb3), this),
                   pl.BlockSpec((tm, d_swq), prev)],
        out_shape=[jax.ShapeDtypeStruct((t, d_sb3), BF16),
                   jax.ShapeDtypeStruct((t, d_swq), BF16)],
        scratch_shapes=[pltpu.VMEM((2, tm, d_swq), BF16),
                        pltpu.VMEM((2, WINDOW + tm, LANES), BF16),
                        pltpu.VMEM((2, WINDOW + tm, LANES), BF16),
                        pltpu.VMEM((tm, LANES), F32), pltpu.VMEM((tm, LANES), F32)],
        compiler_params=_cparams(("arbitrary",)),
        name="inproj",
    )(sinks, x2, mod3, g, w_bf, pos_dense, freq, sign)


def _dot_nt(a, b):
    return lax.dot_general(a, b, (((1,), (1,)), ((), ())), preferred_element_type=F32)


def _sb_block(qm, k, v, tri, carry, causal):
    z = _dot_nt(qm, k)
    log_beta = jnp.minimum(z, 0.0) - jnp.log(1.0 + jnp.exp(-jnp.abs(z)))
    log_rest = log_beta - z
    if causal is not None:
        log_rest = jnp.where(causal, log_rest, 0.0)
    later = jnp.dot(log_rest.astype(BF16), tri, preferred_element_type=F32)
    w = jnp.exp(later + (log_beta + carry))
    if causal is not None:
        w = jnp.where(causal, w, 0.0)
    out = jnp.dot(w.astype(BF16), v, preferred_element_type=F32)
    return out, carry + jnp.sum(log_rest, axis=1, keepdims=True)


def _sb_kernel(blk, n_q, q_ref, k_ref, v_ref, tri_ref, o_ref):
    i = pl.program_id(2)
    tri = tri_ref[...]
    lane = lax.broadcasted_iota(jnp.int32, (blk, LANES), 1)
    row = lax.broadcasted_iota(jnp.int32, (2 * blk, blk), 0)
    col = lax.broadcasted_iota(jnp.int32, (2 * blk, blk), 1)
    causal = col < (row % blk)

    def sweep(qs, kb, carry, causal_mask):
        s = pl.multiple_of(kb * blk, blk)
        return _sb_block(qs, k_ref[0, pl.ds(s, blk), :], v_ref[0, pl.ds(s, blk), :], tri, carry, causal_mask)

    dense = []
    for j in range(n_q):
        g = i * n_q + j
        q = q_ref[0, j * blk:(j + 1) * blk, :]
        qs = jnp.concatenate([jnp.where((lane >= h * HEAD_DIM) & (lane < (h + 1) * HEAD_DIM), q, jnp.zeros_like(q))
                              for h in range(2)], axis=0)
        acc, carry = sweep(qs, g, jnp.zeros((2 * blk, 1), F32), causal)
        for left in range(1, SB_DENSE_LEFT + 1):
            out, carry = sweep(qs, jnp.maximum(g - left, 0), jnp.where(g >= left, carry, -1e30), None)
            acc = acc + out
        dense.append((g, qs, carry, acc))

    for j, (g, qs, carry, acc) in enumerate(dense):
        def cond(state):
            kb, carry, _ = state
            return jnp.logical_and(kb >= 0, jnp.max(carry) >= SB_SKIP_LOG)

        def body(state, qs=qs):
            kb, carry, acc = state
            out, carry = sweep(qs, kb, carry, None)
            return kb - 1, carry, acc + out

        _, _, acc = lax.while_loop(cond, body, (g - 1 - SB_DENSE_LEFT, carry, acc))
        o_ref[0, j * blk:(j + 1) * blk, :] = jnp.where(lane < HEAD_DIM, acc[:blk], acc[blk:]).astype(o_ref.dtype)


def _sb_attention(qkv3, tri):
    b, s, d3 = qkv3.shape
    d_sb = d3 // 3
    n_pairs = d_sb // LANES
    blk = SB_BLOCK
    n_q = SB_BLOCKS_PER_STEP
    return pl.pallas_call(
        functools.partial(_sb_kernel, blk, n_q),
        grid=(b, n_pairs, s // (blk * n_q)),
        in_specs=[pl.BlockSpec((1, blk * n_q, LANES), lambda bi, p, i: (bi, i, p)),
                  pl.BlockSpec((1, s, LANES), lambda bi, p, i: (bi, 0, n_pairs + p)),
                  pl.BlockSpec((1, s, LANES), lambda bi, p, i: (bi, 0, 2 * n_pairs + p)),
                  pl.BlockSpec((blk, blk), lambda bi, p, i: (0, 0))],
        out_specs=pl.BlockSpec((1, blk * n_q, LANES), lambda bi, p, i: (bi, i, p)),
        out_shape=jax.ShapeDtypeStruct((b, s, d_sb), BF16),
        compiler_params=_cparams(("parallel", "parallel", "arbitrary")),
        name="sb_attn",
    )(qkv3, qkv3, qkv3, tri)


def _first_max(vals):
    m = vals[0]
    for v in vals[1:]:
        m = jnp.maximum(m, v)
    idx = jnp.full(m.shape, len(vals) - 1, jnp.int32)
    for k in range(len(vals) - 2, -1, -1):
        idx = jnp.where(vals[k] == m, k, idx)
    return m, idx


def _route(lg):
    g = [lg[k:k + 1, :] for k in range(N_GROUPS)]
    gmax, gsel = _first_max(g)
    gsum = g[0] * 0.0
    for k in range(N_GROUPS):
        gsum = gsum + jnp.exp(g[k] - gmax)
    gprob = 1.0 / gsum
    within = []
    for k in range(EXPERTS_PER_GROUP):
        v = lg[N_GROUPS + (N_GROUPS - 1) * EXPERTS_PER_GROUP + k:N_GROUPS + (N_GROUPS - 1) * EXPERTS_PER_GROUP + k + 1, :]
        for grp in range(N_GROUPS - 2, -1, -1):
            r = N_GROUPS + grp * EXPERTS_PER_GROUP + k
            v = jnp.where(gsel == grp, lg[r:r + 1, :], v)
        within.append(v)
    m1, i1 = _first_max(within)
    rest = [jnp.where(i1 == k, -jnp.inf, within[k]) for k in range(EXPERTS_PER_GROUP)]
    m2, i2 = _first_max(rest)
    e2 = jnp.exp(m2 - m1)
    p1 = gprob / (1.0 + e2)
    p2 = gprob * e2 / (1.0 + e2)
    pair = jnp.zeros_like(i1)
    w_a = jnp.zeros_like(p1)
    w_b = jnp.zeros_like(p1)
    for p, (a, b) in enumerate(PAIR_ROLES):
        fwd = (i1 == a) & (i2 == b)
        rev = (i1 == b) & (i2 == a)
        pair = jnp.where(fwd | rev, p, pair)
        w_a = jnp.where(fwd, p1, jnp.where(rev, p2, w_a))
        w_b = jnp.where(fwd, p2, jnp.where(rev, p1, w_b))
    return gsel * N_PAIRS + pair, w_a, w_b


ROW_TILE = 8


def _write_row_tiles(buf, h2, w_rows):
    tm, d = h2.shape
    words_per_row = d // 2
    hi = pltpu.bitcast(h2[:, :words_per_row].astype(BF16).astype(F32), jnp.uint32)
    lo = pltpu.bitcast(h2[:, words_per_row:].astype(BF16).astype(F32), jnp.uint32)
    words = hi | (lo >> 16)
    n_word_rows = words_per_row // LANES
    for s in range(n_word_rows):
        buf[pl.ds(s, tm, stride=ROW_TILE), :] = words[:, s * LANES:(s + 1) * LANES]
    buf[pl.ds(n_word_rows, tm, stride=ROW_TILE), :] = pltpu.bitcast(w_rows.T, jnp.uint32)
    for s in range(n_word_rows + 1, ROW_TILE):
        buf[pl.ds(s, tm, stride=ROW_TILE), :] = jnp.zeros((tm, LANES), jnp.uint32)


def _ffn_input(x1, mod_ref, gffn_ref):
    shift2 = mod_ref[0, 3:4, :]
    scale2 = mod_ref[0, 4:5, :]
    return (_rms(x1) * gffn_ref[...]) * (1.0 + scale2) + shift2


def _outproj_kernel(d_sb, osb_ref, osw_ref, x_ref, mod_ref, gsb_ref, gsw_ref, wo_ref, gffn_ref, wr_ref, tri_ref,
                    x1_ref, meta_ref, wts_ref, cnt_ref, run_ref):
    i = pl.program_id(0)
    tm = x_ref.shape[0]

    @pl.when(i == 0)
    def _():
        run_ref[...] = jnp.zeros_like(run_ref)

    gate1 = mod_ref[0, 2:3, :]
    a = (_rms(osb_ref[...].astype(F32)) * gsb_ref[...]).astype(BF16)
    bsw = (_rms(osw_ref[...].astype(F32)) * gsw_ref[...]).astype(BF16)
    mix = jnp.dot(a, wo_ref[:d_sb, :], preferred_element_type=F32) + jnp.dot(bsw, wo_ref[d_sb:, :],
                                                                                preferred_element_type=F32)
    x1 = x_ref[...] + gate1 * mix
    x1_ref[...] = x1
    h2 = _ffn_input(x1, mod_ref, gffn_ref)
    h2_hi = h2.astype(BF16)
    h2_lo = (h2 - h2_hi.astype(F32)).astype(BF16)
    wr2 = wr_ref[...]
    by_hi = _dot_nt(wr2, h2_hi)
    lg = by_hi[:META_ROWS] + by_hi[META_ROWS:] + _dot_nt(wr2[:META_ROWS], h2_lo)
    bucket, w_a, w_b = _route(lg)
    rows = lax.broadcasted_iota(jnp.int32, (META_ROWS, tm), 0)
    onehot = rows == bucket
    prefix = jnp.dot(jnp.where(onehot, 1.0, 0.0).astype(BF16), tri_ref[...], preferred_element_type=F32)
    run = run_ref[...]
    rank = jnp.sum(jnp.where(onehot, prefix - 1.0 + run, 0.0), axis=0, keepdims=True)
    run = run + jnp.sum(jnp.where(onehot, 1.0, 0.0), axis=1, keepdims=True)
    run_ref[...] = run
    cnt_ref[...] = jnp.broadcast_to(run, cnt_ref.shape)
    meta_ref[...] = jnp.concatenate([bucket, rank.astype(jnp.int32), jnp.zeros((6, tm), jnp.int32)], axis=0)
    wts_ref[...] = jnp.concatenate([w_a, w_b, jnp.zeros((6, tm), F32)], axis=0)


def _outproj(o_sb, o_sw, x2, mod3, g_sb, g_sw, wo_bf, g_ffn, wr, tri, tiles_per_batch):
    t, d = x2.shape
    d_sb = o_sb.shape[1]
    d_sw = o_sw.shape[1]
    tm = TM_ROUTE
    const = lambda shape: pl.BlockSpec(shape, lambda i: tuple(0 for _ in shape))
    return pl.pallas_call(
        functools.partial(_outproj_kernel, d_sb),
        grid=(t // tm,),
        in_specs=[pl.BlockSpec((tm, d_sb), lambda i: (i, 0)),
                  pl.BlockSpec((tm, d_sw), lambda i: (i, 0)),
                  pl.BlockSpec((tm, d), lambda i: (i, 0)),
                  pl.BlockSpec((1, 6, d), lambda i: (i // tiles_per_batch, 0, 0)),
                  const((1, d_sb)), const((1, d_sw)), const((d_sb + d_sw, d)), const((1, d)),
                  const((2 * META_ROWS, d)), const((tm, tm))],
        out_specs=[pl.BlockSpec((tm, d), lambda i: (i, 0)),
                   pl.BlockSpec((8, tm), lambda i: (0, i)),
                   pl.BlockSpec((8, tm), lambda i: (0, i)),
                   const((META_ROWS, LANES))],
        out_shape=[jax.ShapeDtypeStruct((t, d), F32),
                   jax.ShapeDtypeStruct((8, t), jnp.int32),
                   jax.ShapeDtypeStruct((8, t), F32),
                   jax.ShapeDtypeStruct((META_ROWS, LANES), F32)],
        scratch_shapes=[pltpu.VMEM((META_ROWS, 1), F32)],
        compiler_params=_cparams(("arbitrary",)),
        name="outproj",
    )(o_sb, o_sw, x2, mod3, g_sb, g_sw, wo_bf, g_ffn, wr, tri)


def _tile_rows(row):
    start = row * ROW_TILE
    return pl.ds(start if isinstance(row, int) else pl.multiple_of(start, ROW_TILE), ROW_TILE)


def _tile_copy(src_ref, src_row, dst_ref, dst_row, sem):
    return pltpu.make_async_copy(src_ref.at[_tile_rows(src_row), :], dst_ref.at[_tile_rows(dst_row), :], sem)


def _for_rows_with_index(n_rows, read_index, act, group=8):
    def body(g, c):
        rows = [g * group + u for u in range(group)]
        idx = [read_index(r) for r in rows]
        for u, (r, ix) in enumerate(zip(rows, idx)):
            act(r, ix, u % 2)
        return c
    lax.fori_loop(0, n_rows // group, body, 0)


def _wait_tiles(like_ref, hbm_ref, sem):
    n = like_ref.shape[0]
    pltpu.make_async_copy(hbm_ref.at[pl.ds(0, n), :], like_ref, sem).wait()


def _dispatch_kernel(dest_ref, zblk_ref, x1_ref, mod_ref, gffn_ref, wts_ref, out_ref,
                     rows, zero_blk, sems, zero_sem):
    i = pl.program_id(0)
    n = pl.num_programs(0)
    tm = x1_ref.shape[0]
    slot = i % 2
    buf = rows.at[slot]

    @pl.when(i == 0)
    def _():
        zero_blk[...] = jnp.zeros_like(zero_blk)
        blk_rows = zero_blk.shape[0]

        def each_listed_block(act):
            def step(e, c):
                z = zblk_ref[e]

                @pl.when(z >= 0)
                def _():
                    dst = out_ref.at[pl.ds(pl.multiple_of(z * blk_rows, blk_rows), blk_rows), :]
                    act(pltpu.make_async_copy(zero_blk, dst, zero_sem))
                return c
            lax.fori_loop(0, zblk_ref.shape[0], step, 0)

        each_listed_block(lambda cp: cp.start())
        each_listed_block(lambda cp: cp.wait())

    @pl.when(i >= 2)
    def _():
        _wait_tiles(buf, out_ref, sems.at[slot])

    h2 = _ffn_input(x1_ref[...], mod_ref, gffn_ref)
    _write_row_tiles(buf, h2, jnp.concatenate([wts_ref[...], jnp.zeros((LANES - 8, tm), F32)], axis=0))

    def issue(r, dst_row, queue):
        _tile_copy(buf, r, out_ref, dst_row, sems.at[slot]).start(priority=queue)

    _for_rows_with_index(tm, lambda r: dest_ref[i * tm + r], issue)

    @pl.when(i == n - 1)
    def _():
        _wait_tiles(buf, out_ref, sems.at[slot])

        @pl.when(n >= 2)
        def _():
            _wait_tiles(rows.at[1 - slot], out_ref, sems.at[1 - slot])


def _dispatch(dest, zero_blocks, x1, mod3, g_ffn, wts, p_rows, tiles_per_batch):
    t, d = x1.shape
    tm = TM_ROUTE
    return pl.pallas_call(
        _dispatch_kernel,
        grid_spec=pltpu.PrefetchScalarGridSpec(
            num_scalar_prefetch=2,
            grid=(t // tm,),
            in_specs=[pl.BlockSpec((tm, d), lambda i, *_: (i, 0)),
                      pl.BlockSpec((1, 6, d), lambda i, *_: (i // tiles_per_batch, 0, 0)),
                      pl.BlockSpec((1, d), lambda i, *_: (0, 0)),
                      pl.BlockSpec((8, tm), lambda i, *_: (0, i))],
            out_specs=pl.BlockSpec(memory_space=pl.ANY),
            scratch_shapes=[pltpu.VMEM((2, tm * ROW_TILE, LANES), jnp.uint32),
                            pltpu.VMEM((ROW_BLOCK * ROW_TILE, LANES), jnp.uint32),
                            pltpu.SemaphoreType.DMA((2,)), pltpu.SemaphoreType.DMA(())]),
        out_shape=jax.ShapeDtypeStruct((p_rows * ROW_TILE, LANES), jnp.uint32),
        compiler_params=_cparams(("arbitrary",)),
        name="dispatch",
    )(dest, zero_blocks, x1, mod3, g_ffn, wts)


def _experts_kernel(d, ea_ref, eb_ref, nvalid_ref, x_ref, wga_ref, wua_ref, wda_ref, wgb_ref, wub_ref, wdb_ref, y_ref):
    j = pl.program_id(0)
    rb = x_ref.shape[0] // ROW_TILE
    n_word_rows = d // 2 // LANES

    @pl.when(j < nvalid_ref[0])
    def _():
        high, low = [], []
        for s in range(n_word_rows):
            w = x_ref[pl.ds(s, rb, stride=ROW_TILE), :]
            high.append(pltpu.bitcast(w & jnp.uint32(0xFFFF0000), F32).astype(BF16))
            low.append(pltpu.bitcast(w << 16, F32).astype(BF16))
        xb = jnp.concatenate(high + low, axis=1)
        wts = pltpu.bitcast(x_ref[pl.ds(n_word_rows, rb, stride=ROW_TILE), :], F32)

        def hidden(wg_ref, wu_ref, wrow):
            gt = jnp.dot(xb, wg_ref[0].astype(BF16), preferred_element_type=F32)
            up = jnp.dot(xb, wu_ref[0].astype(BF16), preferred_element_type=F32)
            return ((gt * jax.nn.sigmoid(gt)) * up * wrow).astype(BF16)

        ha = hidden(wga_ref, wua_ref, wts[:, 0:1])
        hb = hidden(wgb_ref, wub_ref, wts[:, 1:2])
        y = (jnp.dot(ha, wda_ref[0].astype(BF16), preferred_element_type=F32)
             + jnp.dot(hb, wdb_ref[0].astype(BF16), preferred_element_type=F32))
        for c in range(ROW_TILE):
            y_ref[pl.ds(c, rb, stride=ROW_TILE), :] = y[:, c * LANES:(c + 1) * LANES]

    @pl.when(j >= nvalid_ref[0])
    def _():
        y_ref[...] = jnp.zeros_like(y_ref)


def _experts(blk_ea, blk_eb, nvalid, sorted_rows, wg_bf, wu_bf, wd_bf):
    n_e, d, d_e = wg_bf.shape
    assert d == ROW_TILE * LANES, "one expert output row must fill one (8, 128) f32 tile"
    rb = ROW_BLOCK
    nb = sorted_rows.shape[0] // (rb * ROW_TILE)
    up_a = pl.BlockSpec((1, d, d_e), lambda j, ea, eb, nv: (ea[j], 0, 0))
    up_b = pl.BlockSpec((1, d, d_e), lambda j, ea, eb, nv: (eb[j], 0, 0))
    dn_a = pl.BlockSpec((1, d_e, d), lambda j, ea, eb, nv: (ea[j], 0, 0))
    dn_b = pl.BlockSpec((1, d_e, d), lambda j, ea, eb, nv: (eb[j], 0, 0))
    rows_in = rows_out = pl.BlockSpec((rb * ROW_TILE, LANES), lambda j, ea, eb, nv: (j, 0))
    return pl.pallas_call(
        functools.partial(_experts_kernel, d),
        grid_spec=pltpu.PrefetchScalarGridSpec(
            num_scalar_prefetch=3,
            grid=(nb,),
            in_specs=[rows_in, up_a, up_a, dn_a, up_b, up_b, dn_b],
            out_specs=rows_out),
        out_shape=jax.ShapeDtypeStruct(sorted_rows.shape, F32),
        compiler_params=_cparams(("arbitrary",)),
        name="experts",
    )(blk_ea, blk_eb, nvalid, sorted_rows, wg_bf, wu_bf, wd_bf, wg_bf, wu_bf, wd_bf)


def _combine_kernel(dest_ref, x1_ref, mod_ref, g_ref, y_hbm, o_ref, ybuf, sems):
    i = pl.program_id(0)
    n = pl.num_programs(0)
    tm = x1_ref.shape[0]
    slot = i % 2

    def issue(tile, slot_):
        def one(r, src_row, queue):
            _tile_copy(y_hbm, src_row, ybuf.at[slot_], r, sems.at[slot_]).start(priority=queue)
        _for_rows_with_index(tm, lambda r: dest_ref[tile * tm + r], one)

    @pl.when(i == 0)
    def _():
        issue(0, 0)

    @pl.when(i + 1 < n)
    def _():
        issue(i + 1, 1 - slot)

    _wait_tiles(ybuf.at[slot], y_hbm, sems.at[slot])
    y = jnp.concatenate([ybuf[slot, pl.ds(c, tm, stride=ROW_TILE), :] for c in range(ROW_TILE)], axis=1)
    gate2 = mod_ref[0, 5:6, :]
    o_ref[...] = _rms(x1_ref[...] + gate2 * y) * g_ref[...]


def _combine(dest, x1, mod3, g_final, y_sorted, tiles_per_batch):
    t, d = x1.shape
    tm = TM_COMB
    return pl.pallas_call(
        _combine_kernel,
        grid_spec=pltpu.PrefetchScalarGridSpec(
            num_scalar_prefetch=1,
            grid=(t // tm,),
            in_specs=[pl.BlockSpec((tm, d), lambda i, dest: (i, 0)),
                      pl.BlockSpec((1, 6, d), lambda i, dest: (i // tiles_per_batch, 0, 0)),
                      pl.BlockSpec((1, d), lambda i, dest: (0, 0)),
                      pl.BlockSpec(memory_space=pl.ANY)],
            out_specs=pl.BlockSpec((tm, d), lambda i, dest: (i, 0)),
            scratch_shapes=[pltpu.VMEM((2, tm * ROW_TILE, LANES), F32), pltpu.SemaphoreType.DMA((2,))]),
        out_shape=jax.ShapeDtypeStruct((t, d), F32),
        compiler_params=_cparams(("arbitrary",)),
        name="combine",
    )(dest, x1, mod3, g_final, y_sorted)


def _pair_table():
    ea = [g * EXPERTS_PER_GROUP + a for g in range(N_GROUPS) for a, _ in PAIR_ROLES]
    eb = [g * EXPERTS_PER_GROUP + b for g in range(N_GROUPS) for _, b in PAIR_ROLES]
    return jnp.array(ea, jnp.int32), jnp.array(eb, jnp.int32)


def _layer(x, mod3, positions, norm_mix_g, w_in, sinks, out_norm_sb_g, out_norm_sw_g, w_out, norm_ffn_g,
           w_router_group, w_router_expert, w_gate, w_up, w_down, norm_final_g):
    b, s, d = x.shape
    t = b * s
    d_sb = N_SB_HEADS * HEAD_DIM
    d_sw = N_SW_HEADS * HEAD_DIM
    scale = HEAD_DIM ** -0.5
    x2 = x.reshape(t, d)

    n_tiles = d_sw // LANES
    sw_heads = jnp.array([h for j in range(n_tiles) for h in (j, j + n_tiles)], jnp.int32)
    sw_cols = (sw_heads[:, None] * HEAD_DIM + jnp.arange(HEAD_DIM, dtype=jnp.int32)[None, :]).reshape(-1)
    q_sw_cols = 3 * d_sb + sw_cols
    w_bf = jnp.concatenate([w_in[:, :d_sb] * scale, w_in[:, d_sb:3 * d_sb],
                            w_in[:, q_sw_cols] * scale, w_in[:, 3 * d_sb + d_sw:]], axis=1).astype(BF16)

    inv_freq = ROPE_THETA ** (-jnp.arange(0, HEAD_DIM, 2, dtype=F32) / HEAD_DIM)
    freq = jnp.tile(inv_freq, LANES // (HEAD_DIM // 2)).reshape(1, LANES)
    sign = jnp.tile(jnp.concatenate([-jnp.ones(HEAD_DIM // 2, F32), jnp.ones(HEAD_DIM // 2, F32)]),
                    LANES // HEAD_DIM).reshape(1, LANES)
    n_freq = HEAD_DIM // 2
    per_row = LANES // n_freq
    pos_dense = jnp.repeat(positions.reshape(t // per_row, per_row), n_freq, axis=1)

    qkv_sb, o_sw = _inproj(sinks, x2, mod3, norm_mix_g.reshape(1, d), w_bf, pos_dense, freq, sign, s // TM_PROJ,
                           3 * d_sb, d_sw)

    tri_sb = (jnp.arange(SB_BLOCK)[:, None] > jnp.arange(SB_BLOCK)[None, :]).astype(BF16)
    o_sb = _sb_attention(qkv_sb.reshape(b, s, 3 * d_sb), tri_sb).reshape(t, d_sb)

    wo_bf = jnp.concatenate([w_out[:d_sb], w_out[d_sb + sw_cols]], axis=0).astype(BF16)
    wr = jnp.concatenate([w_router_group.T, w_router_expert.T,
                          jnp.zeros((META_ROWS - N_GROUPS - N_GROUPS * EXPERTS_PER_GROUP, d), F32)], axis=0)
    wr_hi = wr.astype(BF16)
    wr = jnp.concatenate([wr_hi, (wr - wr_hi.astype(F32)).astype(BF16)], axis=0)
    tri_rt = (jnp.arange(TM_ROUTE)[:, None] <= jnp.arange(TM_ROUTE)[None, :]).astype(BF16)
    g_ffn = norm_ffn_g.reshape(1, d)
    x1, meta, wts, cnt = _outproj(o_sb, o_sw, x2, mod3, out_norm_sb_g.reshape(1, d_sb),
                                  out_norm_sw_g[sw_cols].reshape(1, d_sw), wo_bf, g_ffn, wr, tri_rt, s // TM_ROUTE)

    counts = cnt[:N_BUCKETS, 0].astype(jnp.int32)
    padded = (counts + ROW_BLOCK - 1) // ROW_BLOCK * ROW_BLOCK
    pend = jnp.cumsum(padded)
    pstart = pend - padded
    bucket_ids = jnp.arange(N_BUCKETS, dtype=jnp.int32)[:, None]
    dest = jnp.sum(jnp.where(meta[0][None, :] == bucket_ids, pstart[:, None], 0), axis=0) + meta[1]
    p_rows = t + N_BUCKETS * ROW_BLOCK
    nb = p_rows // ROW_BLOCK
    blk_start = jnp.arange(nb, dtype=jnp.int32) * ROW_BLOCK
    blk_bucket = jnp.minimum(jnp.sum((pend[None, :] <= blk_start[:, None]).astype(jnp.int32), axis=1), N_BUCKETS - 1)
    e_lo, e_hi = _pair_table()
    nvalid = (pend[-1:] // ROW_BLOCK).astype(jnp.int32)

    last_blk = jnp.where(padded > 0, pend // ROW_BLOCK - 1, -1)
    spare_blk = nvalid + jnp.arange(N_BUCKETS, dtype=jnp.int32)
    zero_blocks = jnp.concatenate([last_blk, jnp.where(spare_blk < nb, spare_blk, -1)]).astype(jnp.int32)
    sorted_rows = _dispatch(dest, zero_blocks, x1, mod3, g_ffn, wts, p_rows, s // TM_ROUTE)
    y_sorted = _experts(e_lo[blk_bucket], e_hi[blk_bucket], nvalid, sorted_rows,
                        w_gate, w_up, w_down)
    return _combine(dest, x1, mod3, norm_final_g.reshape(1, d), y_sorted, s // TM_COMB).reshape(b, s, d)


def kernel(x, c, positions, w_ada, b_ada, norm_mix_g, w_in, sinks, out_norm_sb_g, out_norm_sw_g, w_out, norm_ffn_g,
           w_router_group, w_router_expert, w_gate, w_up, w_down, norm_final_g):
    b, s, d = x.shape
    depth = w_ada.shape[0]
    assert depth == 1, "the final norm is fused into the layer's last stage"
    c_pad = jnp.pad(c, ((0, 8 - b), (0, 0)))
    mod = _adaln(c_pad, w_ada[0], b_ada[0].reshape(1, -1))
    mod3 = mod[:b].reshape(b, 6, d)
    return _layer(x, mod3, positions, norm_mix_g[0], w_in[0], sinks[0], out_norm_sb_g[0], out_norm_sw_g[0],
                  w_out[0], norm_ffn_g[0], w_router_group[0], w_router_expert[0], w_gate[0], w_up[0], w_down[0],
                  norm_final_g)
```
